```python
import math
import jax, jax.numpy as jnp
from jax import lax
import numpy as np

D_MODEL = 1024
BATCH = 8
SEQ = 4096
DEPTH = 2
DEC_BATCH = 32
DEC_SEQ = 4
PAST_LEN = 16384
PAGE_SIZE = 128

N_MIXERS = 2
N_SB_LAYERS = (DEPTH + 1) // 2
N_SSM_LAYERS = DEPTH // 2
RMS_EPS = 1e-6

SB_HEADS = 16
SB_HEAD_DIM = D_MODEL // SB_HEADS
SB_SCALE = 1.0 / math.sqrt(SB_HEAD_DIM)
SB_BLOCK = 128
SB_BIAS_INIT = -6.0

SSM_D_INNER = 2 * D_MODEL
SSM_HEAD_DIM = 64
SSM_HEADS = SSM_D_INNER // SSM_HEAD_DIM
SSM_GROUPS = 4
SSM_HEADS_PER_GROUP = SSM_HEADS // SSM_GROUPS
SSM_D_STATE = 128
SSM_CONV_W = 4
SSM_CONV_DIM = SSM_D_INNER + 2 * SSM_GROUPS * SSM_D_STATE
SSM_IN_DIM = SSM_D_INNER + SSM_CONV_DIM + SSM_HEADS
SSM_CHUNK = 128

MOE_GROUPS = 4
MOE_EXPERTS_PER_GROUP = 8
MOE_EXPERTS = MOE_GROUPS * MOE_EXPERTS_PER_GROUP
MOE_TOP_K = 2
MOE_D_FF = D_MODEL // 2
MOE_BLOCK_ROWS = 128

kernel_name = 'stick_breaking_ssd_hier_moe_step'


def rms_norm(x, gain):
    xf = x.astype(jnp.float32)
    y = xf * lax.rsqrt(jnp.mean(xf * xf, axis=-1, keepdims=True) + RMS_EPS)
    return (y * gain.astype(jnp.float32)).astype(x.dtype)


def project_qkv(x_n, w_qkv):
    b, l, _ = x_n.shape
    qkv = (x_n @ w_qkv).reshape(b, l, 3, SB_HEADS, SB_HEAD_DIM)
    return qkv[:, :, 0], qkv[:, :, 1], qkv[:, :, 2]


def sb_accumulate(carry, q, k, v, bias, q_pos, k_pos):
    log_surv, acc = carry
    z = jnp.einsum('bhqd,bkhd->bhqk', q, k.astype(jnp.float32)) + bias.astype(jnp.float32)[None, :, None, None]
    visible = k_pos[None, :] < q_pos[:, None]
    log_keep = jnp.where(visible, jax.nn.log_sigmoid(-z), 0.0)
    later = lax.cumsum(log_keep, axis=3, reverse=True) - log_keep
    log_w = jax.nn.log_sigmoid(z) + later + log_surv[..., None]
    w = jnp.where(visible, jnp.exp(log_w), 0.0)
    acc = acc + jnp.einsum('bhqk,bkhd->bhqd', w, v.astype(jnp.float32))
    return (log_surv + jnp.sum(log_keep, axis=-1), acc)


def sb_attention_prompt(x_n, w_qkv, bias, w_o):
    b, s, _ = x_n.shape
    nb = s // SB_BLOCK
    q, k, v = project_qkv(x_n, w_qkv)
    qb = (q.astype(jnp.float32) * SB_SCALE).reshape(b, nb, SB_BLOCK, SB_HEADS, SB_HEAD_DIM).transpose(1, 0, 3, 2, 4)
    kb = k.reshape(b, nb, SB_BLOCK, SB_HEADS, SB_HEAD_DIM).swapaxes(0, 1)
    vb = v.reshape(b, nb, SB_BLOCK, SB_HEADS, SB_HEAD_DIM).swapaxes(0, 1)
    pos = jnp.arange(s, dtype=jnp.int32).reshape(nb, SB_BLOCK)

    def query_block(args):
        q_blk, q_pos = args
        init = (jnp.zeros((b, SB_HEADS, SB_BLOCK), jnp.float32),
                jnp.zeros((b, SB_HEADS, SB_BLOCK, SB_HEAD_DIM), jnp.float32))

        def key_block(carry, blk):
            k_blk, v_blk, k_pos = blk
            carry = lax.cond(k_pos[0] < q_pos[-1],
                             lambda c: sb_accumulate(c, q_blk, k_blk, v_blk, bias, q_pos, k_pos),
                             lambda c: c, carry)
            return carry, None

        (_, acc), _ = lax.scan(key_block, init, (kb, vb, pos), reverse=True)
        return acc

    o = lax.map(query_block, (qb, pos))
    o = o.transpose(1, 0, 3, 2, 4).reshape(b, s, D_MODEL).astype(x_n.dtype)
    n_pages = s // PAGE_SIZE
    k_pages = k.reshape(b, n_pages, PAGE_SIZE, SB_HEADS, SB_HEAD_DIM)
    v_pages = v.reshape(b, n_pages, PAGE_SIZE, SB_HEADS, SB_HEAD_DIM)
    return o @ w_o, k_pages, v_pages


def sb_attention_sample(x_n, cache_k, cache_v, page_table, slot, w_qkv, bias, w_o):
    b, t, _ = x_n.shape
    n_pages = page_table.shape[1]
    q, k, v = project_qkv(x_n, w_qkv)
    qh = (q.astype(jnp.float32) * SB_SCALE).transpose(0, 2, 1, 3)
    q_pos = n_pages * PAGE_SIZE + jnp.arange(t, dtype=jnp.int32)
    init = (jnp.zeros((b, SB_HEADS, t), jnp.float32),
            jnp.zeros((b, SB_HEADS, t, SB_HEAD_DIM), jnp.float32))
    carry = sb_accumulate(init, qh, k, v, bias, q_pos, q_pos)
    page_pos = (jnp.arange(n_pages, dtype=jnp.int32)[:, None] * PAGE_SIZE
                + jnp.arange(PAGE_SIZE, dtype=jnp.int32)[None, :])

    def page_step(carry, blk):
        phys, k_pos = blk
        k_pg = cache_k[slot, phys]
        v_pg = cache_v[slot, phys]
        return sb_accumulate(carry, qh, k_pg, v_pg, bias, q_pos, k_pos), None

    (_, acc), _ = lax.scan(page_step, carry, (page_table.T, page_pos), reverse=True)
    o = acc.transpose(0, 2, 1, 3).reshape(b, t, D_MODEL).astype(x_n.dtype)
    return o @ w_o, k, v


def ssd_chunked(x, dt, a, bm, cm, init_state, chunk):
    b, l, h, p = x.shape
    nc = l // chunk
    g, kk, n = SSM_GROUPS, SSM_HEADS_PER_GROUP, SSM_D_STATE
    xdt = (x * dt[..., None]).reshape(b, nc, chunk, g, kk, p)
    a_cum = jnp.cumsum((dt * a).reshape(b, nc, chunk, g, kk), axis=2)
    bc = bm.reshape(b, nc, chunk, g, n)
    cc = cm.reshape(b, nc, chunk, g, n)
    causal = jnp.tril(jnp.ones((chunk, chunk), dtype=bool))[None, None, :, :, None, None]
    seg = a_cum[:, :, :, None] - a_cum[:, :, None, :]
    decay = jnp.exp(jnp.where(causal, seg, -jnp.inf))
    cb = jnp.einsum('bctgn,bcsgn->bctsg', cc, bc)
    y_diag = jnp.einsum('bctsgk,bcsgkp->bctgkp', cb[..., None] * decay, xdt)
    x_to_end = xdt * jnp.exp(a_cum[:, :, -1:] - a_cum)[..., None]
    chunk_states = jnp.einsum('bctgn,bctgkp->bcgkpn', bc, x_to_end)
    chunk_decay = jnp.exp(a_cum[:, :, -1])

    def step(state, inp):
        cs, cd = inp
        return state * cd[..., None, None] + cs, state

    final, prev = lax.scan(step, init_state.reshape(b, g, kk, p, n),
                           (jnp.moveaxis(chunk_states, 1, 0), jnp.moveaxis(chunk_decay, 1, 0)))
    prev = jnp.moveaxis(prev, 0, 1)
    y_off = jnp.einsum('bctgn,bcgkpn->bctgkp', cc, prev) * jnp.exp(a_cum)[..., None]
    return (y_diag + y_off).reshape(b, l, h, p), final.reshape(b, h, p, n)


def ssm_mixer(x_n, conv_prev, ssm_prev, chunk, w_in, conv_w, conv_b, dt_bias, a_log, d_skip, norm_w, w_out):
    b, l, _ = x_n.shape
    f32 = jnp.float32
    zxbcdt = x_n @ w_in
    z = zxbcdt[..., :SSM_D_INNER]
    xbc = zxbcdt[..., SSM_D_INNER:SSM_D_INNER + SSM_CONV_DIM]
    dt_raw = zxbcdt[..., SSM_D_INNER + SSM_CONV_DIM:]
    xbc_ext = jnp.concatenate([conv_prev.astype(xbc.dtype), xbc], axis=1)
    conv = conv_b.astype(f32) + sum(xbc_ext[:, i:i + l].astype(f32) * conv_w[i].astype(f32)
                                    for i in range(SSM_CONV_W))
    xbc_act = jax.nn.silu(conv)
    new_conv = xbc_ext[:, l:]
    xs = xbc_act[..., :SSM_D_INNER].reshape(b, l, SSM_HEADS, SSM_HEAD_DIM)
    bm = xbc_act[..., SSM_D_INNER:SSM_D_INNER + SSM_GROUPS * SSM_D_STATE].reshape(b, l, SSM_GROUPS, SSM_D_STATE)
    cm = xbc_act[..., SSM_D_INNER + SSM_GROUPS * SSM_D_STATE:].reshape(b, l, SSM_GROUPS, SSM_D_STATE)
    dt = jax.nn.softplus(dt_raw.astype(f32) + dt_bias.astype(f32))
    a = -jnp.exp(a_log.astype(f32))
    y, final = ssd_chunked(xs, dt, a, bm, cm, ssm_prev.astype(f32), chunk)
    y = y + d_skip.astype(f32)[:, None] * xs
    y = y.reshape(b, l, SSM_D_INNER) * jax.nn.silu(z.astype(f32))
    yg = y.reshape(b, l, SSM_GROUPS, SSM_D_INNER // SSM_GROUPS)
    yg = yg * lax.rsqrt(jnp.mean(yg * yg, axis=-1, keepdims=True) + RMS_EPS)
    y = yg.reshape(b, l, SSM_D_INNER) * norm_w.astype(f32)
    return y.astype(x_n.dtype) @ w_out, final, new_conv


def routed_experts(tok, expert_id, weight, w_gu, w_down):
    n_tok = tok.shape[0]
    n_assign = n_tok * MOE_TOP_K
    flat_e = expert_id.reshape(-1).astype(jnp.int32)
    flat_tok = jnp.repeat(jnp.arange(n_tok, dtype=jnp.int32), MOE_TOP_K)
    order = jnp.argsort(flat_e)
    sorted_e = flat_e[order]
    counts = jnp.bincount(flat_e, length=MOE_EXPERTS).astype(jnp.int32)
    padded = (counts + MOE_BLOCK_ROWS - 1) // MOE_BLOCK_ROWS * MOE_BLOCK_ROWS
    padded_end = jnp.cumsum(padded)
    padded_start = padded_end - padded
    start = jnp.cumsum(counts) - counts
    dest_sorted = padded_start[sorted_e] + jnp.arange(n_assign, dtype=jnp.int32) - start[sorted_e]
    n_blocks = -(-n_assign // MOE_BLOCK_ROWS) + MOE_EXPERTS
    row_tok = jnp.zeros((n_blocks * MOE_BLOCK_ROWS,), jnp.int32).at[dest_sorted].set(flat_tok[order])
    block_start = jnp.arange(n_blocks, dtype=jnp.int32) * MOE_BLOCK_ROWS
    block_expert = jnp.minimum(jnp.searchsorted(padded_end, block_start, side='right'), MOE_EXPERTS - 1)
    rows = tok[row_tok].reshape(n_blocks, MOE_BLOCK_ROWS, D_MODEL)

    def expert_block(args):
        xr, e = args
        gate, up = jnp.split(xr @ w_gu[e], 2, axis=-1)
        return (jax.nn.silu(gate) * up) @ w_down[e]

    out_rows = lax.map(expert_block, (rows, block_expert)).reshape(n_blocks * MOE_BLOCK_ROWS, D_MODEL)
    dest = jnp.zeros((n_assign,), jnp.int32).at[order].set(dest_sorted)
    picked = out_rows[dest].reshape(n_tok, MOE_TOP_K, D_MODEL)
    return jnp.einsum('tk,tkd->td', weight, picked.astype(jnp.float32))


def hier_moe(x_n, w_group, b_group, w_route, b_route, w_gu, w_down):
    shape = x_n.shape
    f32 = jnp.float32
    tok = x_n.reshape(-1, D_MODEL)
    n_tok = tok.shape[0]
    group_p = jax.nn.softmax((tok @ w_group).astype(f32) + b_group.astype(f32), axis=-1)
    g_prob, g_idx = lax.top_k(group_p, 1)
    logits = ((tok @ w_route).astype(f32) + b_route.astype(f32)).reshape(n_tok, MOE_GROUPS, MOE_EXPERTS_PER_GROUP)
    in_group = jnp.einsum('tg,tge->te', jax.nn.one_hot(g_idx[:, 0], MOE_GROUPS, dtype=f32), logits)
    e_prob, e_idx = lax.top_k(jax.nn.softmax(in_group, axis=-1), MOE_TOP_K)
    weight = g_prob * e_prob / jnp.sum(e_prob, axis=-1, keepdims=True)
    expert_id = g_idx * MOE_EXPERTS_PER_GROUP + e_idx
    y = routed_experts(tok, expert_id, weight, w_gu, w_down)
    return y.astype(x_n.dtype).reshape(shape)


def setup_inputs(seed: int = 0) -> dict:
    key = jax.random.key(seed)
    ks = iter(jax.random.split(key, 32))
    f32 = jnp.float32
    n_pages = PAST_LEN // PAGE_SIZE
    n_used = DEC_BATCH * n_pages
    n_phys = (5 * n_used + 3) // 4

    def nrm(shape, scale):
        return scale * jax.random.normal(next(ks), shape, f32)

    x_prompt = nrm((BATCH, SEQ, D_MODEL), 1.0)
    x_sample = nrm((DEC_BATCH, DEC_SEQ, D_MODEL), 1.0)
    cache_k = nrm((N_SB_LAYERS, n_phys, PAGE_SIZE, SB_HEADS, SB_HEAD_DIM), 1.0)
    cache_v = nrm((N_SB_LAYERS, n_phys, PAGE_SIZE, SB_HEADS, SB_HEAD_DIM), 1.0)
    page_table = jax.random.permutation(next(ks), n_phys)[:n_used].reshape(DEC_BATCH, n_pages).astype(jnp.int32)
    state_ssm = nrm((N_SSM_LAYERS, DEC_BATCH, SSM_HEADS, SSM_HEAD_DIM, SSM_D_STATE), 0.1)
    state_conv = nrm((N_SSM_LAYERS, DEC_BATCH, SSM_CONV_W - 1, SSM_CONV_DIM), 1.0)
    norm_mix = 1.0 + nrm((DEPTH, D_MODEL), 0.02)
    norm_ffn = 1.0 + nrm((DEPTH, D_MODEL), 0.02)
    norm_final = 1.0 + nrm((D_MODEL,), 0.02)
    sb_w_qkv = nrm((N_SB_LAYERS, D_MODEL, 3 * D_MODEL), D_MODEL ** -0.5)
    sb_bias = SB_BIAS_INIT + nrm((N_SB_LAYERS, SB_HEADS), 0.5)
    sb_w_o = nrm((N_SB_LAYERS, D_MODEL, D_MODEL), D_MODEL ** -0.5)
    ssm_w_in = nrm((N_SSM_LAYERS, D_MODEL, SSM_IN_DIM), D_MODEL ** -0.5)
    ssm_conv_w = nrm((N_SSM_LAYERS, SSM_CONV_W, SSM_CONV_DIM), SSM_CONV_W ** -0.5)
    ssm_conv_b = nrm((N_SSM_LAYERS, SSM_CONV_DIM), 0.01)
    dt0 = jnp.exp(jax.random.uniform(next(ks), (N_SSM_LAYERS, SSM_HEADS), f32, math.log(1e-3), math.log(1e-1)))
    ssm_dt_bias = dt0 + jnp.log(-jnp.expm1(-dt0))
    ssm_a_log = jnp.log(jax.random.uniform(next(ks), (N_SSM_LAYERS, SSM_HEADS), f32, 1.0, 16.0))
    ssm_d = 1.0 + nrm((N_SSM_LAYERS, SSM_HEADS), 0.1)
    ssm_norm_w = 1.0 + nrm((N_SSM_LAYERS, SSM_D_INNER), 0.02)
    ssm_w_out = nrm((N_SSM_LAYERS, SSM_D_INNER, D_MODEL), SSM_D_INNER ** -0.5)
    moe_w_group = nrm((DEPTH, D_MODEL, MOE_GROUPS), D_MODEL ** -0.5)
    moe_b_group = nrm((DEPTH, MOE_GROUPS), 0.01)
    moe_w_route = nrm((DEPTH, D_MODEL, MOE_EXPERTS), D_MODEL ** -0.5)
    moe_b_route = nrm((DEPTH, MOE_EXPERTS), 0.01)
    moe_w_gu = nrm((DEPTH, MOE_EXPERTS, D_MODEL, 2 * MOE_D_FF), D_MODEL ** -0.5)
    moe_w_down = nrm((DEPTH, MOE_EXPERTS, MOE_D_FF, D_MODEL), MOE_D_FF ** -0.5)
    return {'x_prompt': x_prompt, 'x_sample': x_sample, 'cache_k': cache_k, 'cache_v': cache_v,
            'page_table': page_table, 'state_ssm': state_ssm, 'state_conv': state_conv,
            'norm_mix': norm_mix, 'norm_ffn': norm_ffn, 'norm_final': norm_final,
            'sb_w_qkv': sb_w_qkv, 'sb_bias': sb_bias, 'sb_w_o': sb_w_o,
            'ssm_w_in': ssm_w_in, 'ssm_conv_w': ssm_conv_w, 'ssm_conv_b': ssm_conv_b,
            'ssm_dt_bias': ssm_dt_bias, 'ssm_a_log': ssm_a_log, 'ssm_d': ssm_d,
            'ssm_norm_w': ssm_norm_w, 'ssm_w_out': ssm_w_out,
            'moe_w_group': moe_w_group, 'moe_b_group': moe_b_group,
            'moe_w_route': moe_w_route, 'moe_b_route': moe_b_route,
            'moe_w_gu': moe_w_gu, 'moe_w_down': moe_w_down}


def reference(x_prompt, x_sample, cache_k, cache_v, page_table, state_ssm, state_conv,
              norm_mix, norm_ffn, norm_final, sb_w_qkv, sb_bias, sb_w_o,
              ssm_w_in, ssm_conv_w, ssm_conv_b, ssm_dt_bias, ssm_a_log, ssm_d, ssm_norm_w, ssm_w_out,
              moe_w_group, moe_b_group, moe_w_route, moe_b_route, moe_w_gu, moe_w_down):
    hp, hs = x_prompt, x_sample
    k_p, v_p, k_s, v_s = [], [], [], []
    ssm_p, conv_p, ssm_s, conv_s = [], [], [], []
    for layer in range(DEPTH):
        slot = layer // N_MIXERS
        xp_n = rms_norm(hp, norm_mix[layer])
        xs_n = rms_norm(hs, norm_mix[layer])
        if layer % N_MIXERS == 0:
            yp, kp, vp = sb_attention_prompt(xp_n, sb_w_qkv[slot], sb_bias[slot], sb_w_o[slot])
            ys, kn, vn = sb_attention_sample(xs_n, cache_k, cache_v, page_table, slot,
                                             sb_w_qkv[slot], sb_bias[slot], sb_w_o[slot])
            k_p.append(kp); v_p.append(vp); k_s.append(kn); v_s.append(vn)
        else:
            params = (ssm_w_in[slot], ssm_conv_w[slot], ssm_conv_b[slot], ssm_dt_bias[slot],
                      ssm_a_log[slot], ssm_d[slot], ssm_norm_w[slot], ssm_w_out[slot])
            b = hp.shape[0]
            conv0 = jnp.zeros((b, SSM_CONV_W - 1, SSM_CONV_DIM), hp.dtype)
            ssm0 = jnp.zeros((b, SSM_HEADS, SSM_HEAD_DIM, SSM_D_STATE), jnp.float32)
            yp, sp, cp = ssm_mixer(xp_n, conv0, ssm0, SSM_CHUNK, *params)
            ys, sn, cn = ssm_mixer(xs_n, state_conv[slot], state_ssm[slot], hs.shape[1], *params)
            ssm_p.append(sp); conv_p.append(cp); ssm_s.append(sn); conv_s.append(cn)
        hp = hp + yp
        hs = hs + ys
        moe_params = (moe_w_group[layer], moe_b_group[layer], moe_w_route[layer], moe_b_route[layer],
                      moe_w_gu[layer], moe_w_down[layer])
        hp = hp + hier_moe(rms_norm(hp, norm_ffn[layer]), *moe_params)
        hs = hs + hier_moe(rms_norm(hs, norm_ffn[layer]), *moe_params)
    y_prompt = rms_norm(hp, norm_final)
    y_sample = rms_norm(hs, norm_final)
    k_prompt = jnp.stack(k_p)
    v_prompt = jnp.stack(v_p)
    k_sample = jnp.stack(k_s)
    v_sample = jnp.stack(v_s)
    ssm_prompt = jnp.stack(ssm_p)
    conv_prompt = jnp.stack(conv_p)
    ssm_sample = jnp.stack(ssm_s)
    conv_sample = jnp.stack(conv_s)
    return (y_prompt, y_sample, k_prompt, v_prompt, k_sample, v_sample, ssm_prompt, conv_prompt, ssm_sample, conv_sample)
```

```python
import functools
import math

import jax
import jax.numpy as jnp
from jax import lax
from jax.experimental import pallas as pl
from jax.experimental.pallas import tpu as pltpu

F32 = jnp.float32
BF16 = jnp.bfloat16

D_MODEL = 1024
RMS_EPS = 1e-6
N_MIXERS = 2

SB_HEADS = 16
SB_HEAD_DIM = D_MODEL // SB_HEADS
SB_SCALE = 1.0 / math.sqrt(SB_HEAD_DIM)
SB_BLOCK = 128
PAGE_SIZE = 128
PAGES_PER_STEP = 8

SSM_D_INNER = 2 * D_MODEL
SSM_HEAD_DIM = 64
SSM_HEADS = SSM_D_INNER // SSM_HEAD_DIM
SSM_GROUPS = 4
SSM_HEADS_PER_GROUP = SSM_HEADS // SSM_GROUPS
SSM_GROUP_DIM = SSM_D_INNER // SSM_GROUPS
SSM_D_STATE = 128
SSM_CONV_W = 4
SSM_CONV_DIM = SSM_D_INNER + 2 * SSM_GROUPS * SSM_D_STATE
SSM_CHUNK = 128

MOE_GROUPS = 4
MOE_EXPERTS_PER_GROUP = 8
MOE_EXPERTS = MOE_GROUPS * MOE_EXPERTS_PER_GROUP
MOE_TOP_K = 2
MOE_D_FF = D_MODEL // 2
MOE_BLOCK_ROWS = 256
ROUTE_ROWS = 48

LANES = 128
VMEM_LIMIT_BYTES = 48 * 1024 * 1024

NT_DIMS = (((1,), (1,)), ((), ()))


def _params(*semantics):
    return pltpu.CompilerParams(dimension_semantics=semantics, vmem_limit_bytes=VMEM_LIMIT_BYTES)


def _rms_norm(x, gain):
    return x * lax.rsqrt(jnp.mean(x * x, axis=-1, keepdims=True) + RMS_EPS) * gain


def _split2(x):
    hi = x.astype(BF16)
    lo = (x - hi.astype(F32)).astype(BF16)
    return hi, lo


def _split3(x):
    hi = x.astype(BF16)
    r = x - hi.astype(F32)
    mid = r.astype(BF16)
    lo = (r - mid.astype(F32)).astype(BF16)
    return hi, mid, lo


def _softplus(z):
    return jnp.maximum(z, 0.0) + jnp.log(1.0 + jnp.exp(-jnp.abs(z)))


def _sigmoid(x):
    return 1.0 / (1.0 + jnp.exp(-x))


def _row_tile(t, want):
    return want if t % want == 0 else t


def _qkv_kernel(x_ref, g_ref, wqv_ref, wkvt_ref, q_ref, kt_ref, vt_ref, ktb_ref, vb_ref):
    xn = _rms_norm(x_ref[...], g_ref[...]).astype(BF16)
    q = jnp.dot(xn, wqv_ref[:, 0:D_MODEL], preferred_element_type=F32)
    v = jnp.dot(xn, wqv_ref[:, D_MODEL:2 * D_MODEL], preferred_element_type=F32)
    kt = lax.dot_general(wkvt_ref[0:D_MODEL, :], xn, NT_DIMS, preferred_element_type=F32)
    vt = lax.dot_general(wkvt_ref[D_MODEL:2 * D_MODEL, :], xn, NT_DIMS, preferred_element_type=F32)
    q_ref[...] = (q * SB_SCALE).astype(BF16)
    vb_ref[...] = v.astype(BF16)
    for p in range(kt_ref.shape[0]):
        cols = slice(p * PAGE_SIZE, (p + 1) * PAGE_SIZE)
        kt_ref[p] = kt[:, cols]
        vt_ref[p] = vt[:, cols]
        ktb_ref[p] = kt[:, cols].astype(BF16)


def qkv_project(h, gain, w_qv, w_kv_t):
    t = h.shape[0]
    tm = _row_tile(t, 256)
    pages = tm // PAGE_SIZE
    row = lambda i: (i, 0)
    fixed = lambda i: (0, 0)
    page = lambda i: (i, 0, 0)
    return pl.pallas_call(
        _qkv_kernel,
        grid=(t // tm,),
        in_specs=[pl.BlockSpec((tm, D_MODEL), row),
                  pl.BlockSpec((1, D_MODEL), fixed),
                  pl.BlockSpec((D_MODEL, 2 * D_MODEL), fixed),
                  pl.BlockSpec((2 * D_MODEL, D_MODEL), fixed)],
        out_specs=[pl.BlockSpec((tm, D_MODEL), row),
                   pl.BlockSpec((pages, D_MODEL, PAGE_SIZE), page),
                   pl.BlockSpec((pages, D_MODEL, PAGE_SIZE), page),
                   pl.BlockSpec((pages, D_MODEL, PAGE_SIZE), page),
                   pl.BlockSpec((tm, D_MODEL), row)],
        out_shape=[jax.ShapeDtypeStruct((t, D_MODEL), BF16),
                   jax.ShapeDtypeStruct((t // PAGE_SIZE, D_MODEL, PAGE_SIZE), F32),
                   jax.ShapeDtypeStruct((t // PAGE_SIZE, D_MODEL, PAGE_SIZE), F32),
                   jax.ShapeDtypeStruct((t // PAGE_SIZE, D_MODEL, PAGE_SIZE), BF16),
                   jax.ShapeDtypeStruct((t, D_MODEL), BF16)],
        compiler_params=_params("parallel"),
        name="qkv_project",
    )(h, gain, w_qv, w_kv_t)


SSM_DT_COLS = SSM_GROUPS * LANES


def _ssm_in_kernel(x_ref, g_ref, wz_ref, wx_ref, wd_ref, z_ref, xbc_ref, dt_ref):
    xn = _rms_norm(x_ref[...], g_ref[...]).astype(BF16)
    z_ref[...] = jnp.dot(xn, wz_ref[...], preferred_element_type=F32)
    xbc_ref[...] = jnp.dot(xn, wx_ref[...], preferred_element_type=F32)
    dt_ref[...] = jnp.dot(xn, wd_ref[...], preferred_element_type=F32)


def ssm_in_project(h, gain, w_z, w_xbc, w_dt):
    t = h.shape[0]
    tm = _row_tile(t, 256)
    row = lambda i: (i, 0)
    fixed = lambda i: (0, 0)
    return pl.pallas_call(
        _ssm_in_kernel,
        grid=(t // tm,),
        in_specs=[pl.BlockSpec((tm, D_MODEL), row),
                  pl.BlockSpec((1, D_MODEL), fixed),
                  pl.BlockSpec((D_MODEL, SSM_D_INNER), fixed),
                  pl.BlockSpec((D_MODEL, SSM_CONV_DIM), fixed),
                  pl.BlockSpec((D_MODEL, SSM_DT_COLS), fixed)],
        out_specs=[pl.BlockSpec((tm, SSM_D_INNER), row),
                   pl.BlockSpec((tm, SSM_CONV_DIM), row),
                   pl.BlockSpec((tm, SSM_DT_COLS), row)],
        out_shape=[jax.ShapeDtypeStruct((t, SSM_D_INNER), F32),
                   jax.ShapeDtypeStruct((t, SSM_CONV_DIM), F32),
                   jax.ShapeDtypeStruct((t, SSM_DT_COLS), F32)],
        compiler_params=_params("parallel"),
        name="ssm_in_project",
    )(h, gain, w_z, w_xbc, w_dt)


def _out_proj_kernel(h_ref, a_ref, w_ref, o_ref):
    o_ref[...] = h_ref[...] + jnp.dot(a_ref[...], w_ref[...], preferred_element_type=F32)


def out_project_residual(h, a_bf16, w_bf16):
    t, k = a_bf16.shape
    tm = _row_tile(t, 512)
    row = lambda i: (i, 0)
    return pl.pallas_call(
        _out_proj_kernel,
        grid=(t // tm,),
        in_specs=[pl.BlockSpec((tm, D_MODEL), row),
                  pl.BlockSpec((tm, k), row),
                  pl.BlockSpec((k, D_MODEL), lambda i: (0, 0))],
        out_specs=pl.BlockSpec((tm, D_MODEL), row),
        out_shape=jax.ShapeDtypeStruct((t, D_MODEL), F32),
        compiler_params=_params("parallel"),
        name="out_project_residual",
    )(h, a_bf16, w_bf16)


def _sb_prompt_kernel(q_ref, kt_ref, v_ref, bias_ref, tri_ref, o_ref, surv_ref, acc_ref):
    blk = SB_BLOCK
    qi = pl.program_id(2)
    q = q_ref[0]
    bias = bias_ref[0]
    first_lanes = lax.broadcasted_iota(jnp.int32, (blk, LANES), 1) < SB_HEAD_DIM
    first_rows = lax.broadcasted_iota(jnp.int32, (LANES, blk), 0) < SB_HEAD_DIM
    row = lax.broadcasted_iota(jnp.int32, (blk, 2 * blk), 0)
    col = lax.broadcasted_iota(jnp.int32, (blk, 2 * blk), 1)
    strictly_earlier = (col & (blk - 1)) < row

    surv_ref[...] = jnp.zeros_like(surv_ref)
    acc_ref[...] = jnp.zeros_like(acc_ref)

    def fold(j, diagonal):
        start = pl.multiple_of(j * blk, blk)
        kt = kt_ref[j]
        v = v_ref[0, pl.ds(start, blk), :]
        zero = jnp.zeros_like(kt)
        kt2 = jnp.concatenate([jnp.where(first_rows, kt, zero), jnp.where(first_rows, zero, kt)], axis=1)
        v2 = jnp.concatenate([jnp.where(first_lanes, v, zero), jnp.where(first_lanes, zero, v)], axis=0)
        z = jnp.dot(q, kt2, preferred_element_type=F32) + bias
        sp = _softplus(z)
        if diagonal:
            sp = jnp.where(strictly_earlier, sp, 0.0)
        hi, lo = _split2(sp)
        ra = jnp.dot(jnp.concatenate([hi[:, :blk], lo[:, :blk]], axis=1), tri_ref[...],
                     preferred_element_type=F32)
        rb = jnp.dot(jnp.concatenate([hi[:, blk:], lo[:, blk:]], axis=1), tri_ref[...],
                     preferred_element_type=F32)
        suffix = jnp.concatenate([ra[:, :blk], rb[:, :blk]], axis=1)
        total = jnp.concatenate([ra[:, blk:], rb[:, blk:]], axis=1)
        w = jnp.exp(z - suffix - surv_ref[...])
        if diagonal:
            w = jnp.where(strictly_earlier, w, 0.0)
        acc_ref[...] += jnp.dot(w.astype(BF16), v2, preferred_element_type=F32)
        surv_ref[...] += total

    fold(qi, True)

    def body(jj, carry):
        fold(qi - 1 - jj, False)
        return carry

    lax.fori_loop(0, qi, body, 0)
    o_ref[0] = acc_ref[...].astype(o_ref.dtype)


def _suffix_sum_matrix(blk):
    j = jnp.arange(blk)[:, None]
    s = jnp.arange(blk)[None, :]
    tri = (j >= s).astype(BF16)
    half = jnp.concatenate([tri, jnp.ones((blk, blk), BF16)], axis=1)
    return jnp.concatenate([half, half], axis=0)


def sb_attention_prompt(q, ktb, vb, bias):
    b, s, _ = q.shape
    blk = SB_BLOCK
    assert blk == PAGE_SIZE
    n_pairs = SB_HEADS // 2
    bias2 = jnp.repeat(bias.astype(F32).reshape(n_pairs, 2), blk, axis=1).reshape(n_pairs, 1, 2 * blk)
    return pl.pallas_call(
        _sb_prompt_kernel,
        grid=(b, n_pairs, s // blk),
        in_specs=[pl.BlockSpec((1, blk, LANES), lambda bi, hp, qi: (bi, qi, hp)),
                  pl.BlockSpec((s // blk, LANES, blk), lambda bi, hp, qi: (bi, hp, 0)),
                  pl.BlockSpec((1, s, LANES), lambda bi, hp, qi: (bi, 0, hp)),
                  pl.BlockSpec((1, 1, 2 * blk), lambda bi, hp, qi: (hp, 0, 0)),
                  pl.BlockSpec((2 * blk, 2 * blk), lambda bi, hp, qi: (0, 0))],
        out_specs=pl.BlockSpec((1, blk, LANES), lambda bi, hp, qi: (bi, qi, hp)),
        out_shape=jax.ShapeDtypeStruct((b, s, D_MODEL), BF16),
        scratch_shapes=[pltpu.VMEM((blk, 2 * blk), F32), pltpu.VMEM((blk, LANES), F32)],
        compiler_params=_params("parallel", "parallel", "arbitrary"),
        name="sb_attention_prompt",
    )(q, ktb, vb, bias2, _suffix_sum_matrix(blk))


def _sb_sample_kernel(n_new, pt_ref, qbd_ref, bias_ref, kn_ref, vn_ref, tri_ref, *rest):
    n_pg = PAGES_PER_STEP
    k_refs = rest[:n_pg]
    v_refs = rest[n_pg:2 * n_pg]
    o_ref, surv_ref, acc_ref = rest[2 * n_pg:]
    step = pl.program_id(1)
    qbd = qbd_ref[0]
    bias = bias_ref[...]
    pg = PAGE_SIZE
    rows = qbd.shape[0]

    def fold(kt, vt, visible):
        z = jnp.dot(qbd, kt, preferred_element_type=F32) + bias
        sp = _softplus(z)
        if visible is not None:
            sp = jnp.where(visible, sp, 0.0)
        hi, lo = _split2(sp)
        r = jnp.dot(jnp.concatenate([hi, lo], axis=1), tri_ref[...], preferred_element_type=F32)
        w = jnp.exp(z - r[:, :pg] - surv_ref[...])
        if visible is not None:
            w = jnp.where(visible, w, 0.0)
        acc_ref[...] += lax.dot_general(w.astype(BF16), vt, NT_DIMS, preferred_element_type=F32)
        surv_ref[...] += r[:, pg:]

    @pl.when(step == 0)
    def _():
        surv_ref[...] = jnp.zeros_like(surv_ref)
        acc_ref[...] = jnp.zeros_like(acc_ref)
        query = lax.broadcasted_iota(jnp.int32, (rows, pg), 0) // SB_HEADS
        key = lax.broadcasted_iota(jnp.int32, (rows, pg), 1)
        fold(kn_ref[0], vn_ref[0], (key < query) & (key < n_new))

    for i in range(n_pg):
        fold(k_refs[i][0].astype(BF16), v_refs[i][0].astype(BF16), None)

    @pl.when(step == pl.num_programs(1) - 1)
    def _():
        head_of_row = lax.broadcasted_iota(jnp.int32, (SB_HEADS, D_MODEL), 0)
        head_of_lane = lax.broadcasted_iota(jnp.int32, (SB_HEADS, D_MODEL), 1) // SB_HEAD_DIM
        out = []
        for t in range(o_ref.shape[1]):
            if t < n_new:
                blk = acc_ref[t * SB_HEADS:(t + 1) * SB_HEADS, :]
                out.append(jnp.sum(jnp.where(head_of_row == head_of_lane, blk, 0.0), axis=0, keepdims=True))
            else:
                out.append(jnp.zeros((1, D_MODEL), F32))
        o_ref[0] = jnp.concatenate(out, axis=0).astype(o_ref.dtype)


def sb_attention_sample(q, kt_new, vt_new, cache_kt, cache_vt, page_table, bias):
    b, t, _ = q.shape
    n_pages = page_table.shape[1]
    n_pg = PAGES_PER_STEP
    rows = t * SB_HEADS
    assert n_pages % n_pg == 0 and rows % 16 == 0 and t <= 8
    t_pad = 8
    q4 = q.reshape(b, t, SB_HEADS, SB_HEAD_DIM)
    eye = jnp.eye(SB_HEADS, dtype=BF16)
    qbd = jnp.einsum("bthd,hg->bthgd", q4, eye).reshape(b, rows, D_MODEL)
    bias_rows = jnp.tile(bias.astype(F32), t).reshape(rows, 1)
    new_pages = lambda a: jnp.pad(a.reshape(D_MODEL, b, t).transpose(1, 0, 2),
                                  ((0, 0), (0, 0), (0, PAGE_SIZE - t))).astype(BF16)

    def page_map(i):
        return lambda bi, si, pt: (pt[bi * n_pages + (n_pages - 1 - (si * n_pg + i))], 0, 0)

    per_b = lambda bi, si, pt: (bi, 0, 0)
    fixed = lambda bi, si, pt: (0, 0)
    page_specs = [pl.BlockSpec((1, D_MODEL, PAGE_SIZE), page_map(i)) for i in range(n_pg)]
    grid_spec = pltpu.PrefetchScalarGridSpec(
        num_scalar_prefetch=1,
        grid=(b, n_pages // n_pg),
        in_specs=[pl.BlockSpec((1, rows, D_MODEL), per_b),
                  pl.BlockSpec((rows, 1), fixed),
                  pl.BlockSpec((1, D_MODEL, PAGE_SIZE), per_b),
                  pl.BlockSpec((1, D_MODEL, PAGE_SIZE), per_b),
                  pl.BlockSpec((2 * PAGE_SIZE, 2 * PAGE_SIZE), fixed)] + page_specs + page_specs,
        out_specs=pl.BlockSpec((1, t_pad, D_MODEL), per_b),
        scratch_shapes=[pltpu.VMEM((rows, PAGE_SIZE), F32), pltpu.VMEM((rows, D_MODEL), F32)],
    )
    o = pl.pallas_call(
        functools.partial(_sb_sample_kernel, t),
        grid_spec=grid_spec,
        out_shape=jax.ShapeDtypeStruct((b, t_pad, D_MODEL), BF16),
        compiler_params=_params("parallel", "arbitrary"),
        name="sb_attention_sample",
    )(page_table.reshape(-1).astype(jnp.int32), qbd, bias_rows, new_pages(kt_new), new_pages(vt_new),
      _suffix_sum_matrix(PAGE_SIZE), *([cache_kt] * n_pg), *([cache_vt] * n_pg))
    return o[:, :t]


def _ssd_kernel(valid_len, has_init,
                x_ref, bm_ref, cm_ref, z_ref, dt_ref,
                cwx_ref, cwb_ref, cwc_ref, cbx_ref, cbb_ref, cbc_ref,
                px_ref, pb_ref, pc_ref, init_ref,
                dtb_ref, a_ref, aexp_ref, dexp_ref, nw_ref, expand_ref,
                y_ref, fin_ref,
                ext_x, ext_b, ext_c, state_ref):
    lc = x_ref.shape[1]
    c = pl.program_id(2)
    tail = 8

    @pl.when(c == 0)
    def _():
        ext_x[0:tail, :] = px_ref[0]
        ext_b[0:tail, :] = pb_ref[0]
        ext_c[0:tail, :] = pc_ref[0]
        if has_init:
            n = SSM_HEADS_PER_GROUP * SSM_HEAD_DIM
            state_ref[...] = init_ref[0].reshape(n, SSM_D_STATE).T
        else:
            state_ref[...] = jnp.zeros_like(state_ref)

    def conv_silu(cur_ref, ext, w_ref, b_ref):
        ext[tail:tail + lc, :] = cur_ref[0]
        acc = b_ref[...] + w_ref[SSM_CONV_W - 1:SSM_CONV_W, :] * ext[tail:tail + lc, :]
        for back in range(1, SSM_CONV_W):
            tap = SSM_CONV_W - 1 - back
            acc = acc + w_ref[tap:tap + 1, :] * ext[tail - back:tail - back + lc, :]
        ext[0:tail, :] = ext[lc:lc + tail, :]
        return acc * _sigmoid(acc)

    xs = conv_silu(x_ref, ext_x, cwx_ref, cbx_ref)
    bm = conv_silu(bm_ref, ext_b, cwb_ref, cbb_ref)
    cm = conv_silu(cm_ref, ext_c, cwc_ref, cbc_ref)

    dt = _softplus(dt_ref[0] + dtb_ref[0])
    if valid_len < lc:
        dt = jnp.where(lax.broadcasted_iota(jnp.int32, dt.shape, 0) < valid_len, dt, 0.0)
    t_idx = lax.broadcasted_iota(jnp.int32, (lc, lc), 0)
    s_idx = lax.broadcasted_iota(jnp.int32, (lc, lc), 1)
    causal = s_idx <= t_idx
    lower = causal.astype(BF16)

    def cumsum_rows(v):
        parts = _split3(v)
        return sum(jnp.dot(lower, p, preferred_element_type=F32) for p in parts)

    a_cum = cumsum_rows(dt * a_ref[0])
    a_cum_t = a_cum.T
    expand = expand_ref[...]
    dt_wide = sum(jnp.dot(p, expand, preferred_element_type=F32) for p in _split3(dt))
    a_cum_wide = cumsum_rows(dt_wide * aexp_ref[...])
    a_last_wide = a_cum_wide[lc - 1:lc, :]

    bm_t = bm.T.astype(BF16)
    cm_b = cm.astype(BF16)
    cb = jnp.dot(cm_b, bm_t, preferred_element_type=F32)
    xdt = xs * dt_wide
    xdt_b = xdt.astype(BF16)
    first_half = lax.broadcasted_iota(jnp.int32, (lc, LANES), 1) < SSM_HEAD_DIM
    y_parts = []
    for pair in range(SSM_HEADS_PER_GROUP // 2):
        halves = []
        for k in (2 * pair, 2 * pair + 1):
            seg = a_cum[:, k:k + 1] - a_cum_t[k:k + 1, :]
            m = (cb * jnp.exp(jnp.where(causal, seg, -jnp.inf))).astype(BF16)
            halves.append(jnp.dot(m, xdt_b[:, pair * LANES:(pair + 1) * LANES], preferred_element_type=F32))
        y_parts.append(jnp.where(first_half, halves[0], halves[1]))
    y = jnp.concatenate(y_parts, axis=1)

    state = state_ref[...]
    y = y + jnp.dot(cm_b, state.astype(BF16), preferred_element_type=F32) * jnp.exp(a_cum_wide)
    y = y + dexp_ref[...] * xs
    y = y * (z_ref[0] * _sigmoid(z_ref[0]))
    y = y * lax.rsqrt(jnp.mean(y * y, axis=-1, keepdims=True) + RMS_EPS) * nw_ref[...]
    y_ref[0] = y.astype(y_ref.dtype)

    x_to_end = (xdt * jnp.exp(a_last_wide - a_cum_wide)).astype(BF16)
    new_state = state * jnp.exp(a_last_wide) + jnp.dot(bm_t, x_to_end, preferred_element_type=F32)
    state_ref[...] = new_state

    @pl.when(c == pl.num_programs(2) - 1)
    def _():
        fin_ref[0] = new_state.T.reshape(SSM_HEADS_PER_GROUP, SSM_HEAD_DIM, SSM_D_STATE)


def ssd_mixer(z, xbc, dt, conv_prev, init_state, valid_len, p):
    b, l, _ = z.shape
    lc = SSM_CHUNK
    assert l % lc == 0
    n_x = SSM_D_INNER // SSM_GROUP_DIM
    gd, ns = SSM_GROUP_DIM, SSM_D_STATE
    b_off = SSM_D_INNER // ns
    c_off = b_off + SSM_GROUPS
    has_init = init_state is not None
    if not has_init:
        init_state = jnp.zeros((1, SSM_HEADS, SSM_HEAD_DIM, ns), F32)

    seq = lambda off: (lambda bi, g, c: (bi, c, off + g))
    par = lambda off: (lambda bi, g, c: (0, off + g))
    prev = lambda off: (lambda bi, g, c: (bi, 0, off + g))
    init_map = (lambda bi, g, c: (bi, g, 0, 0)) if has_init else (lambda bi, g, c: (0, g, 0, 0))
    hg = SSM_HEADS_PER_GROUP
    in_specs = [
        pl.BlockSpec((1, lc, gd), seq(0)), pl.BlockSpec((1, lc, ns), seq(b_off)), pl.BlockSpec((1, lc, ns), seq(c_off)),
        pl.BlockSpec((1, lc, gd), seq(0)), pl.BlockSpec((1, lc, LANES), seq(0)),
        pl.BlockSpec((SSM_CONV_W, gd), par(0)), pl.BlockSpec((SSM_CONV_W, ns), par(b_off)),
        pl.BlockSpec((SSM_CONV_W, ns), par(c_off)),
        pl.BlockSpec((1, gd), par(0)), pl.BlockSpec((1, ns), par(b_off)), pl.BlockSpec((1, ns), par(c_off)),
        pl.BlockSpec((1, 8, gd), prev(0)), pl.BlockSpec((1, 8, ns), prev(b_off)), pl.BlockSpec((1, 8, ns), prev(c_off)),
        pl.BlockSpec((1, hg, SSM_HEAD_DIM, ns), init_map),
        pl.BlockSpec((1, 1, LANES), lambda bi, g, c: (g, 0, 0)),
        pl.BlockSpec((1, 1, LANES), lambda bi, g, c: (g, 0, 0)),
        pl.BlockSpec((1, gd), par(0)), pl.BlockSpec((1, gd), par(0)), pl.BlockSpec((1, gd), par(0)),
        pl.BlockSpec((LANES, gd), lambda bi, g, c: (0, 0)),
    ]
    out_specs = [pl.BlockSpec((1, lc, gd), seq(0)),
                 pl.BlockSpec((1, hg, SSM_HEAD_DIM, ns), lambda bi, g, c: (bi, g, 0, 0))]
    return pl.pallas_call(
        functools.partial(_ssd_kernel, valid_len, has_init),
        grid=(b, SSM_GROUPS, l // lc),
        in_specs=in_specs,
        out_specs=out_specs,
        out_shape=[jax.ShapeDtypeStruct((b, l, SSM_D_INNER), BF16),
                   jax.ShapeDtypeStruct((b, SSM_HEADS, SSM_HEAD_DIM, ns), F32)],
        scratch_shapes=[pltpu.VMEM((lc + 8, gd), F32), pltpu.VMEM((lc + 8, ns), F32),
                        pltpu.VMEM((lc + 8, ns), F32), pltpu.VMEM((ns, gd), F32)],
        compiler_params=_params("parallel", "parallel", "arbitrary"),
        name="ssd_mixer",
    )(xbc, xbc, xbc, z, dt,
      p["conv_w"], p["conv_w"], p["conv_w"], p["conv_b"], p["conv_b"], p["conv_b"],
      conv_prev, conv_prev, conv_prev, init_state,
      p["dt_bias"], p["a_neg"], p["a_wide"], p["d_wide"], p["norm_w"], p["expand"])


def ssm_params(w_in, conv_w, conv_b, dt_bias, a_log, d_skip, norm_w, w_out):
    hg = SSM_HEADS_PER_GROUP
    w_dt = w_in[:, SSM_D_INNER + SSM_CONV_DIM:].reshape(D_MODEL, SSM_GROUPS, hg)
    w_dt = jnp.pad(w_dt, ((0, 0), (0, 0), (0, LANES - hg))).reshape(D_MODEL, SSM_DT_COLS)
    lane_pad = lambda v: jnp.pad(v.astype(F32).reshape(SSM_GROUPS, 1, hg), ((0, 0), (0, 0), (0, LANES - hg)))
    a_neg = -jnp.exp(a_log.astype(F32))
    head_of_chan = jnp.arange(SSM_GROUP_DIM) // SSM_HEAD_DIM
    expand = (jnp.arange(LANES)[:, None] == head_of_chan[None, :]).astype(BF16)
    return {
        "w_z": w_in[:, :SSM_D_INNER].astype(BF16),
        "w_xbc": w_in[:, SSM_D_INNER:SSM_D_INNER + SSM_CONV_DIM].astype(BF16),
        "w_dt": w_dt.astype(BF16),
        "conv_w": conv_w.astype(F32), "conv_b": conv_b.astype(F32).reshape(1, SSM_CONV_DIM),
        "dt_bias": lane_pad(dt_bias), "a_neg": lane_pad(a_neg),
        "a_wide": jnp.repeat(a_neg, SSM_HEAD_DIM).reshape(1, SSM_D_INNER),
        "d_wide": jnp.repeat(d_skip.astype(F32), SSM_HEAD_DIM).reshape(1, SSM_D_INNER),
        "norm_w": norm_w.astype(F32).reshape(1, SSM_D_INNER),
        "expand": expand,
        "w_out": w_out.astype(BF16),
    }


def _router_kernel(h_ref, g_ref, w_ref, b_ref, before_ref, xn_ref, meta_ref, cnt_ref, carry_ref):
    step = pl.program_id(0)
    tm = h_ref.shape[0]

    @pl.when(step == 0)
    def _():
        carry_ref[...] = jnp.zeros_like(carry_ref)

    xn = _rms_norm(h_ref[...], g_ref[...])
    xn_ref[...] = xn
    logits = lax.dot_general(w_ref[...], xn.astype(BF16), NT_DIMS, preferred_element_type=F32) + b_ref[...]

    sub8 = lax.broadcasted_iota(jnp.int32, (8, tm), 0)
    neg_inf = jnp.float32(-jnp.inf)

    def first_argmax(v, vmax):
        return jnp.min(jnp.where(v == vmax, sub8, 8), axis=0, keepdims=True)

    grp = jnp.where(sub8 < MOE_GROUPS, logits[0:8], neg_inf)
    g_max = jnp.max(grp, axis=0, keepdims=True)
    g_idx = first_argmax(grp, g_max)
    g_prob = 1.0 / jnp.sum(jnp.exp(grp - g_max), axis=0, keepdims=True)

    in_group = logits[8:16]
    for g in range(1, MOE_GROUPS):
        in_group = jnp.where(g_idx == g, logits[8 + 8 * g:16 + 8 * g], in_group)
    m1 = jnp.max(in_group, axis=0, keepdims=True)
    i1 = first_argmax(in_group, m1)
    rest = jnp.where(sub8 == i1, neg_inf, in_group)
    m2 = jnp.max(rest, axis=0, keepdims=True)
    i2 = first_argmax(rest, m2)
    p2 = jnp.exp(m2 - m1)
    w1 = g_prob / (1.0 + p2)
    w2 = g_prob * p2 / (1.0 + p2)
    e1 = g_idx * MOE_EXPERTS_PER_GROUP + i1
    e2 = g_idx * MOE_EXPERTS_PER_GROUP + i2

    sub_e = lax.broadcasted_iota(jnp.int32, (MOE_EXPERTS, tm), 0)
    hot1 = sub_e == e1
    hot2 = sub_e == e2
    hot = jnp.concatenate([hot1, hot2], axis=0).astype(BF16)
    earlier = jnp.dot(hot, before_ref[...], preferred_element_type=F32)
    hot1f = hot1.astype(F32)
    hot2f = hot2.astype(F32)
    cnt1 = jnp.sum(hot1f, axis=1, keepdims=True)
    cnt2 = jnp.sum(hot2f, axis=1, keepdims=True)
    carry = carry_ref[...]
    r1 = jnp.sum(hot1f * (earlier[:MOE_EXPERTS] + carry), axis=0, keepdims=True)
    r2 = jnp.sum(hot2f * (earlier[MOE_EXPERTS:] + carry + cnt1), axis=0, keepdims=True)
    carry = carry + cnt1 + cnt2
    carry_ref[...] = carry
    cnt_ref[...] = carry
    zero = jnp.zeros((1, tm), F32)
    meta_ref[...] = jnp.concatenate(
        [e1.astype(F32), e2.astype(F32), w1, w2, r1, r2, zero, zero], axis=0)


def moe_route(h, gain, r):
    t = h.shape[0]
    tm = _row_tile(t, 512)
    i = jnp.arange(tm)
    before = (i[:, None] < i[None, :]).astype(BF16)
    fixed = lambda s: (0, 0)
    return pl.pallas_call(
        _router_kernel,
        grid=(t // tm,),
        in_specs=[pl.BlockSpec((tm, D_MODEL), lambda s: (s, 0)),
                  pl.BlockSpec((1, D_MODEL), fixed),
                  pl.BlockSpec((ROUTE_ROWS, D_MODEL), fixed),
                  pl.BlockSpec((ROUTE_ROWS, 1), fixed),
                  pl.BlockSpec((tm, tm), fixed)],
        out_specs=[pl.BlockSpec((tm, D_MODEL), lambda s: (s, 0)),
                   pl.BlockSpec((8, tm), lambda s: (0, s)),
                   pl.BlockSpec((MOE_EXPERTS, 1), fixed)],
        out_shape=[jax.ShapeDtypeStruct((t, D_MODEL), F32),
                   jax.ShapeDtypeStruct((8, t), F32),
                   jax.ShapeDtypeStruct((MOE_EXPERTS, 1), F32)],
        scratch_shapes=[pltpu.VMEM((MOE_EXPERTS, 1), F32)],
        compiler_params=_params("arbitrary"),
        name="moe_route",
    )(h, gain, r["w_route"], r["bias"], before)


def _scatter_rows_kernel(pos_ref, x_ref, rows_in_ref, rows_ref, sem):
    del rows_in_ref
    tm = x_ref.shape[0]

    def copy(k, t):
        return pltpu.make_async_copy(x_ref.at[pl.ds(t, 1), :], rows_ref.at[pl.ds(pos_ref[k, t], 1), :], sem)

    def start(t, carry):
        copy(0, t).start()
        copy(1, t).start()
        return carry

    def wait(t, carry):
        copy(0, t).wait()
        copy(1, t).wait()
        return carry

    lax.fori_loop(0, tm, start, 0)
    lax.fori_loop(0, tm, wait, 0)


def moe_scatter_rows(xn, pos, n_rows):
    t = xn.shape[0]
    tm = _row_tile(t, 512)
    rows0 = jnp.zeros((n_rows, D_MODEL), F32)
    return pl.pallas_call(
        _scatter_rows_kernel,
        grid=(t // tm,),
        in_specs=[pl.BlockSpec((2, tm), lambda s: (0, s), memory_space=pltpu.SMEM),
                  pl.BlockSpec((tm, D_MODEL), lambda s: (s, 0)),
                  pl.BlockSpec(memory_space=pl.ANY)],
        out_specs=pl.BlockSpec(memory_space=pl.ANY),
        out_shape=jax.ShapeDtypeStruct((n_rows, D_MODEL), F32),
        scratch_shapes=[pltpu.SemaphoreType.DMA(())],
        input_output_aliases={2: 0},
        compiler_params=_params("arbitrary"),
        name="moe_scatter_rows",
    )(pos, xn, rows0)


def _expert_mlp_kernel(be_ref, nb_ref, x_ref, wgu_ref, wd_ref, o_ref):
    i = pl.program_id(0)

    @pl.when(i < nb_ref[0])
    def _():
        x = x_ref[...].astype(BF16)
        gate = jnp.dot(x, wgu_ref[0, :, :MOE_D_FF], preferred_element_type=F32)
        up = jnp.dot(x, wgu_ref[0, :, MOE_D_FF:], preferred_element_type=F32)
        act = (gate * _sigmoid(gate) * up).astype(BF16)
        o_ref[...] = jnp.dot(act, wd_ref[0], preferred_element_type=F32)

    @pl.when(i >= nb_ref[0])
    def _():
        o_ref[...] = jnp.zeros_like(o_ref)


def moe_expert_mlp(rows, block_expert, n_used, w_gu, w_down):
    n_rows = rows.shape[0]
    br = MOE_BLOCK_ROWS
    grid_spec = pltpu.PrefetchScalarGridSpec(
        num_scalar_prefetch=2,
        grid=(n_rows // br,),
        in_specs=[pl.BlockSpec((br, D_MODEL), lambda i, be, nb: (i, 0)),
                  pl.BlockSpec((1, D_MODEL, 2 * MOE_D_FF), lambda i, be, nb: (be[i], 0, 0)),
                  pl.BlockSpec((1, MOE_D_FF, D_MODEL), lambda i, be, nb: (be[i], 0, 0))],
        out_specs=pl.BlockSpec((br, D_MODEL), lambda i, be, nb: (i, 0)),
    )
    return pl.pallas_call(
        _expert_mlp_kernel,
        grid_spec=grid_spec,
        out_shape=jax.ShapeDtypeStruct((n_rows, D_MODEL), F32),
        compiler_params=_params("arbitrary"),
        name="moe_expert_mlp",
    )(block_expert, n_used, rows, w_gu, w_down)


def _combine_kernel(final_norm, pos_ref, h_ref, w_ref, rows_ref, g_ref, o_ref, y_ref, buf1, buf2, sem):
    tm = h_ref.shape[0]

    def copies(t):
        return (pltpu.make_async_copy(rows_ref.at[pl.ds(pos_ref[0, t], 1), :], buf1.at[pl.ds(t, 1), :], sem),
                pltpu.make_async_copy(rows_ref.at[pl.ds(pos_ref[1, t], 1), :], buf2.at[pl.ds(t, 1), :], sem))

    def start(t, carry):
        c1, c2 = copies(t)
        c1.start()
        c2.start()
        return carry

    def wait(t, carry):
        c1, c2 = copies(t)
        c1.wait()
        c2.wait()
        return carry

    lax.fori_loop(0, tm, start, 0)
    lax.fori_loop(0, tm, wait, 0)
    w = w_ref[...]
    out = h_ref[...] + (w[:, 0:1] * buf1[...] + w[:, 1:2] * buf2[...])
    o_ref[...] = out
    if final_norm:
        y_ref[...] = _rms_norm(out, g_ref[...])
    else:
        y_ref[...] = jnp.zeros_like(y_ref)


def moe_combine(h, pos, w_cols, out_rows, final_gain):
    t = h.shape[0]
    tm = _row_tile(t, 256)
    final_norm = final_gain is not None
    gain = final_gain if final_norm else jnp.ones((1, D_MODEL), F32)
    y_rows = tm if final_norm else 8
    y_shape = (t, D_MODEL) if final_norm else (8 * (t // tm), D_MODEL)
    out, y = pl.pallas_call(
        functools.partial(_combine_kernel, final_norm),
        grid=(t // tm,),
        in_specs=[pl.BlockSpec((2, tm), lambda s: (0, s), memory_space=pltpu.SMEM),
                  pl.BlockSpec((tm, D_MODEL), lambda s: (s, 0)),
                  pl.BlockSpec((tm, 2), lambda s: (s, 0)),
                  pl.BlockSpec(memory_space=pl.ANY),
                  pl.BlockSpec((1, D_MODEL), lambda s: (0, 0))],
        out_specs=[pl.BlockSpec((tm, D_MODEL), lambda s: (s, 0)),
                   pl.BlockSpec((y_rows, D_MODEL), lambda s: (s, 0))],
        out_shape=[jax.ShapeDtypeStruct((t, D_MODEL), F32), jax.ShapeDtypeStruct(y_shape, F32)],
        scratch_shapes=[pltpu.VMEM((tm, D_MODEL), F32), pltpu.VMEM((tm, D_MODEL), F32),
                        pltpu.SemaphoreType.DMA(())],
        compiler_params=_params("arbitrary"),
        name="moe_combine",
    )(pos, h, w_cols, out_rows, gain)
    return out, (y if final_norm else None)


def moe_params(w_group, b_group, w_route, b_route, w_gu, w_down):
    w_rt = jnp.zeros((ROUTE_ROWS, D_MODEL), F32)
    w_rt = w_rt.at[0:MOE_GROUPS].set(w_group.astype(F32).T)
    w_rt = w_rt.at[8:8 + MOE_EXPERTS].set(w_route.astype(F32).T)
    bias = jnp.zeros((ROUTE_ROWS,), F32)
    bias = bias.at[0:MOE_GROUPS].set(b_group.astype(F32)).at[8:8 + MOE_EXPERTS].set(b_route.astype(F32))
    return {"w_route": w_rt.astype(BF16), "bias": bias.reshape(ROUTE_ROWS, 1),
            "w_gu": w_gu.astype(BF16), "w_down": w_down.astype(BF16)}


def hier_moe_residual(h, gain, r, final_gain=None):
    t = h.shape[0]
    br = MOE_BLOCK_ROWS
    xn, meta, counts = moe_route(h, gain, r)
    counts = counts[:, 0].astype(jnp.int32)
    padded = (counts + br - 1) // br * br
    padded_end = jnp.cumsum(padded)
    padded_start = padded_end - padded
    expert = meta[0:2].astype(jnp.int32)
    hot = expert[..., None] == jnp.arange(MOE_EXPERTS, dtype=jnp.int32)
    pos = jnp.sum(jnp.where(hot, padded_start, 0), axis=-1) + meta[4:6].astype(jnp.int32)
    n_blocks = -(-(t * MOE_TOP_K) // br) + MOE_EXPERTS
    block_start = jnp.arange(n_blocks, dtype=jnp.int32) * br
    block_expert = jnp.minimum(jnp.searchsorted(padded_end, block_start, side="right"), MOE_EXPERTS - 1)
    n_used = (padded_end[-1] // br).reshape(1)
    rows = moe_scatter_rows(xn, pos, n_blocks * br)
    out_rows = moe_expert_mlp(rows, block_expert.astype(jnp.int32), n_used.astype(jnp.int32), r["w_gu"], r["w_down"])
    return moe_combine(h, pos, meta[2:4].T, out_rows, final_gain)


def kernel(x_prompt, x_sample, cache_k, cache_v, page_table, state_ssm, state_conv, norm_mix, norm_ffn, norm_final,
           sb_w_qkv, sb_bias, sb_w_o, ssm_w_in, ssm_conv_w, ssm_conv_b, ssm_dt_bias, ssm_a_log, ssm_d, ssm_norm_w,
           ssm_w_out, moe_w_group, moe_b_group, moe_w_route, moe_b_route, moe_w_gu, moe_w_down):
    bp, sp, _ = x_prompt.shape
    bs, ts, _ = x_sample.shape
    depth = norm_mix.shape[0]
    hp = x_prompt.reshape(bp * sp, D_MODEL)
    hs = x_sample.reshape(bs * ts, D_MODEL)
    n_phys = cache_k.shape[1]
    gain = lambda g: g.astype(F32).reshape(1, D_MODEL)
    outs = {name: [] for name in ("k_p", "v_p", "k_s", "v_s", "ssm_p", "conv_p", "ssm_s", "conv_s")}
    yp = ys = None
    for layer in range(depth):
        slot = layer // N_MIXERS
        g_mix = gain(norm_mix[layer])
        if layer % N_MIXERS == 0:
            w = sb_w_qkv[slot]
            w_qv = jnp.concatenate([w[:, :D_MODEL], w[:, 2 * D_MODEL:]], axis=1).astype(BF16)
            w_kv_t = w[:, D_MODEL:].T.astype(BF16)
            w_o = sb_w_o[slot].astype(BF16)
            q, kt, vt, ktb, vb = qkv_project(hp, g_mix, w_qv, w_kv_t)
            shape3 = (bp, sp, D_MODEL)
            o = sb_attention_prompt(q.reshape(shape3), ktb, vb.reshape(shape3), sb_bias[slot])
            hp = out_project_residual(hp, o.reshape(bp * sp, D_MODEL), w_o)
            pages = lambda a: a.reshape(bp, sp // PAGE_SIZE, SB_HEADS, SB_HEAD_DIM, PAGE_SIZE).transpose(0, 1, 4, 2, 3)
            outs["k_p"].append(pages(kt))
            outs["v_p"].append(pages(vt))

            q, kt, vt, _, _ = qkv_project(hs, g_mix, w_qv, w_kv_t)
            kt, vt = kt[0], vt[0]
            cache_t = lambda c: c[slot].transpose(0, 2, 3, 1).reshape(n_phys, D_MODEL, PAGE_SIZE)
            o = sb_attention_sample(q.reshape(bs, ts, D_MODEL), kt, vt, cache_t(cache_k), cache_t(cache_v),
                                    page_table, sb_bias[slot])
            hs = out_project_residual(hs, o.reshape(bs * ts, D_MODEL), w_o)
            rows = lambda a: a.reshape(SB_HEADS, SB_HEAD_DIM, bs, ts).transpose(2, 3, 0, 1)
            outs["k_s"].append(rows(kt))
            outs["v_s"].append(rows(vt))
        else:
            p = ssm_params(ssm_w_in[slot], ssm_conv_w[slot], ssm_conv_b[slot], ssm_dt_bias[slot], ssm_a_log[slot],
                           ssm_d[slot], ssm_norm_w[slot], ssm_w_out[slot])
            z, xbc, dt = ssm_in_project(hp, g_mix, p["w_z"], p["w_xbc"], p["w_dt"])
            xbc3 = xbc.reshape(bp, sp, SSM_CONV_DIM)
            y, fin = ssd_mixer(z.reshape(bp, sp, SSM_D_INNER), xbc3, dt.reshape(bp, sp, SSM_DT_COLS),
                               jnp.zeros((bp, 8, SSM_CONV_DIM), F32), None, SSM_CHUNK, p)
            hp = out_project_residual(hp, y.reshape(bp * sp, SSM_D_INNER), p["w_out"])
            outs["ssm_p"].append(fin)
            outs["conv_p"].append(xbc3[:, sp - (SSM_CONV_W - 1):])

            z, xbc, dt = ssm_in_project(hs, g_mix, p["w_z"], p["w_xbc"], p["w_dt"])
            pad_seq = lambda a: jnp.pad(a.reshape(bs, ts, -1), ((0, 0), (0, SSM_CHUNK - ts), (0, 0)))
            conv_prev = jnp.pad(state_conv[slot].astype(F32), ((0, 0), (8 - (SSM_CONV_W - 1), 0), (0, 0)))
            y, fin = ssd_mixer(pad_seq(z), pad_seq(xbc), pad_seq(dt), conv_prev, state_ssm[slot].astype(F32), ts, p)
            hs = out_project_residual(hs, y[:, :ts].reshape(bs * ts, SSM_D_INNER), p["w_out"])
            xbc_ext = jnp.concatenate([state_conv[slot].astype(F32), xbc.reshape(bs, ts, SSM_CONV_DIM)], axis=1)
            outs["ssm_s"].append(fin)
            outs["conv_s"].append(xbc_ext[:, ts:])
        r = moe_params(moe_w_group[layer], moe_b_group[layer], moe_w_route[layer], moe_b_route[layer],
                       moe_w_gu[layer], moe_w_down[layer])
        final_gain = gain(norm_final) if layer == depth - 1 else None
        hp, yp = hier_moe_residual(hp, gain(norm_ffn[layer]), r, final_gain)
        hs, ys = hier_moe_residual(hs, gain(norm_ffn[layer]), r, final_gain)
    return (yp.reshape(bp, sp, D_MODEL), ys.reshape(bs, ts, D_MODEL),
            jnp.stack(outs["k_p"]), jnp.stack(outs["v_p"]), jnp.stack(outs["k_s"]), jnp.stack(outs["v_s"]),
            jnp.stack(outs["ssm_p"]), jnp.stack(outs["conv_p"]), jnp.stack(outs["ssm_s"]), jnp.stack(outs["conv_s"]))
```

```python
import functools
import math

import jax
import jax.numpy as jnp
from jax import lax
from jax.experimental import pallas as pl
from jax.experimental.pallas import tpu as pltpu

F32 = jnp.float32
BF16 = jnp.bfloat16

D_MODEL = 1024
RMS_EPS = 1e-6
N_MIXERS = 2

SB_HEADS = 16
SB_HEAD_DIM = D_MODEL // SB_HEADS
SB_SCALE = 1.0 / math.sqrt(SB_HEAD_DIM)
SB_BLOCK = 128
SB_QUERY_TILE = 512
PAGE_SIZE = 128
PAGES_PER_STEP = 8

SSM_D_INNER = 2 * D_MODEL
SSM_HEAD_DIM = 64
SSM_HEADS = SSM_D_INNER // SSM_HEAD_DIM
SSM_GROUPS = 4
SSM_HEADS_PER_GROUP = SSM_HEADS // SSM_GROUPS
SSM_GROUP_DIM = SSM_D_INNER // SSM_GROUPS
SSM_D_STATE = 128
SSM_CONV_W = 4
SSM_CONV_DIM = SSM_D_INNER + 2 * SSM_GROUPS * SSM_D_STATE
SSM_CHUNK = 128

MOE_GROUPS = 4
MOE_EXPERTS_PER_GROUP = 8
MOE_EXPERTS = MOE_GROUPS * MOE_EXPERTS_PER_GROUP
MOE_TOP_K = 2
MOE_D_FF = D_MODEL // 2
MOE_BLOCK_ROWS = 256
ROUTE_ROWS = 48
ROW_DMA_UNROLL = 8

LANES = 128
VMEM_LIMIT_BYTES = 48 * 1024 * 1024

NT_DIMS = (((1,), (1,)), ((), ()))


def _params(*semantics):
    return pltpu.CompilerParams(dimension_semantics=semantics, vmem_limit_bytes=VMEM_LIMIT_BYTES)


def _rms_norm(x, gain):
    return x * lax.rsqrt(jnp.mean(x * x, axis=-1, keepdims=True) + RMS_EPS) * gain


def _split2(x):
    hi = x.astype(BF16)
    lo = (x - hi.astype(F32)).astype(BF16)
    return hi, lo


def _split3(x):
    hi = x.astype(BF16)
    r = x - hi.astype(F32)
    mid = r.astype(BF16)
    lo = (r - mid.astype(F32)).astype(BF16)
    return hi, mid, lo


def _softplus(z):
    return jnp.maximum(z, 0.0) + jnp.log(1.0 + jnp.exp(-jnp.abs(z)))


def _sigmoid(x):
    return 1.0 / (1.0 + jnp.exp(-x))


def _row_tile(t, want):
    return want if t % want == 0 else t


def _qkv_kernel(x_ref, g_ref, wqv_ref, wkvt_ref, q_ref, kt_ref, vt_ref, ktb_ref, vb_ref):
    xn = _rms_norm(x_ref[...], g_ref[...]).astype(BF16)
    q = jnp.dot(xn, wqv_ref[:, 0:D_MODEL], preferred_element_type=F32)
    v = jnp.dot(xn, wqv_ref[:, D_MODEL:2 * D_MODEL], preferred_element_type=F32)
    kt = lax.dot_general(wkvt_ref[0:D_MODEL, :], xn, NT_DIMS, preferred_element_type=F32)
    vt = lax.dot_general(wkvt_ref[D_MODEL:2 * D_MODEL, :], xn, NT_DIMS, preferred_element_type=F32)
    q_ref[...] = (q * SB_SCALE).astype(BF16)
    vb_ref[...] = v.astype(BF16)
    for p in range(kt_ref.shape[0]):
        cols = slice(p * PAGE_SIZE, (p + 1) * PAGE_SIZE)
        kt_ref[p] = kt[:, cols]
        vt_ref[p] = vt[:, cols]
        ktb_ref[p] = kt[:, cols].astype(BF16)


def qkv_project(h, gain, w_qv, w_kv_t):
    t = h.shape[0]
    tm = _row_tile(t, 256)
    pages = tm // PAGE_SIZE
    row = lambda i: (i, 0)
    fixed = lambda i: (0, 0)
    page = lambda i: (i, 0, 0)
    return pl.pallas_call(
        _qkv_kernel,
        grid=(t // tm,),
        in_specs=[pl.BlockSpec((tm, D_MODEL), row),
                  pl.BlockSpec((1, D_MODEL), fixed),
                  pl.BlockSpec((D_MODEL, 2 * D_MODEL), fixed),
                  pl.BlockSpec((2 * D_MODEL, D_MODEL), fixed)],
        out_specs=[pl.BlockSpec((tm, D_MODEL), row),
                   pl.BlockSpec((pages, D_MODEL, PAGE_SIZE), page),
                   pl.BlockSpec((pages, D_MODEL, PAGE_SIZE), page),
                   pl.BlockSpec((pages, D_MODEL, PAGE_SIZE), page),
                   pl.BlockSpec((tm, D_MODEL), row)],
        out_shape=[jax.ShapeDtypeStruct((t, D_MODEL), BF16),
                   jax.ShapeDtypeStruct((t // PAGE_SIZE, D_MODEL, PAGE_SIZE), F32),
                   jax.ShapeDtypeStruct((t // PAGE_SIZE, D_MODEL, PAGE_SIZE), F32),
                   jax.ShapeDtypeStruct((t // PAGE_SIZE, D_MODEL, PAGE_SIZE), BF16),
                   jax.ShapeDtypeStruct((t, D_MODEL), BF16)],
        compiler_params=_params("parallel"),
        name="qkv_project",
    )(h, gain, w_qv, w_kv_t)


SSM_DT_COLS = SSM_GROUPS * LANES


def _ssm_in_kernel(x_ref, g_ref, wz_ref, wx_ref, wd_ref, z_ref, xbc_ref, dt_ref):
    xn = _rms_norm(x_ref[...], g_ref[...]).astype(BF16)
    z_ref[...] = jnp.dot(xn, wz_ref[...], preferred_element_type=F32)
    xbc_ref[...] = jnp.dot(xn, wx_ref[...], preferred_element_type=F32)
    dt_ref[...] = jnp.dot(xn, wd_ref[...], preferred_element_type=F32)


def ssm_in_project(h, gain, w_z, w_xbc, w_dt):
    t = h.shape[0]
    tm = _row_tile(t, 256)
    row = lambda i: (i, 0)
    fixed = lambda i: (0, 0)
    return pl.pallas_call(
        _ssm_in_kernel,
        grid=(t // tm,),
        in_specs=[pl.BlockSpec((tm, D_MODEL), row),
                  pl.BlockSpec((1, D_MODEL), fixed),
                  pl.BlockSpec((D_MODEL, SSM_D_INNER), fixed),
                  pl.BlockSpec((D_MODEL, SSM_CONV_DIM), fixed),
                  pl.BlockSpec((D_MODEL, SSM_DT_COLS), fixed)],
        out_specs=[pl.BlockSpec((tm, SSM_D_INNER), row),
                   pl.BlockSpec((tm, SSM_CONV_DIM), row),
                   pl.BlockSpec((tm, SSM_DT_COLS), row)],
        out_shape=[jax.ShapeDtypeStruct((t, SSM_D_INNER), F32),
                   jax.ShapeDtypeStruct((t, SSM_CONV_DIM), F32),
                   jax.ShapeDtypeStruct((t, SSM_DT_COLS), F32)],
        compiler_params=_params("parallel"),
        name="ssm_in_project",
    )(h, gain, w_z, w_xbc, w_dt)


def _out_proj_kernel(h_ref, a_ref, w_ref, o_ref):
    o_ref[...] = h_ref[...] + jnp.dot(a_ref[...], w_ref[...], preferred_element_type=F32)


def out_project_residual(h, a_bf16, w_bf16):
    t, k = a_bf16.shape
    tm = _row_tile(t, 512)
    row = lambda i: (i, 0)
    return pl.pallas_call(
        _out_proj_kernel,
        grid=(t // tm,),
        in_specs=[pl.BlockSpec((tm, D_MODEL), row),
                  pl.BlockSpec((tm, k), row),
                  pl.BlockSpec((k, D_MODEL), lambda i: (0, 0))],
        out_specs=pl.BlockSpec((tm, D_MODEL), row),
        out_shape=jax.ShapeDtypeStruct((t, D_MODEL), F32),
        compiler_params=_params("parallel"),
        name="out_project_residual",
    )(h, a_bf16, w_bf16)


def _sb_prompt_kernel(q_ref, kt_ref, v_ref, bias_ref, tri_ref, o_ref, surv_ref, acc_ref):
    blk = SB_BLOCK
    tq = q_ref.shape[1]
    band = tq // blk
    qi = pl.program_id(2)
    bias = bias_ref[0]
    first_lanes = lax.broadcasted_iota(jnp.int32, (blk, LANES), 1) < SB_HEAD_DIM
    first_rows = lax.broadcasted_iota(jnp.int32, (LANES, blk), 0) < SB_HEAD_DIM

    surv_ref[...] = jnp.zeros_like(surv_ref)
    acc_ref[...] = jnp.zeros_like(acc_ref)

    def fold(j, band_block):
        start = pl.multiple_of(j * blk, blk)
        kt = kt_ref[j]
        v = v_ref[0, pl.ds(start, blk), :]
        zero = jnp.zeros_like(kt)
        kt2 = jnp.concatenate([jnp.where(first_rows, kt, zero), jnp.where(first_rows, zero, kt)], axis=1)
        v2 = jnp.concatenate([jnp.where(first_lanes, v, zero), jnp.where(first_lanes, zero, v)], axis=0)
        z = jnp.dot(q_ref[0], kt2, preferred_element_type=F32) + bias
        sp = _softplus(z)
        if band_block is not None:
            query = lax.broadcasted_iota(jnp.int32, (tq, 2 * blk), 0)
            key = (lax.broadcasted_iota(jnp.int32, (tq, 2 * blk), 1) & (blk - 1)) + band_block * blk
            strictly_earlier = key < query
            sp = jnp.where(strictly_earlier, sp, 0.0)
        hi, lo = _split2(sp)
        ra = jnp.dot(jnp.concatenate([hi[:, :blk], lo[:, :blk]], axis=1), tri_ref[...],
                     preferred_element_type=F32)
        rb = jnp.dot(jnp.concatenate([hi[:, blk:], lo[:, blk:]], axis=1), tri_ref[...],
                     preferred_element_type=F32)
        suffix = jnp.concatenate([ra[:, :blk], rb[:, :blk]], axis=1)
        total = jnp.concatenate([ra[:, blk:], rb[:, blk:]], axis=1)
        w = jnp.exp(z - suffix - surv_ref[...])
        if band_block is not None:
            w = jnp.where(strictly_earlier, w, 0.0)
        acc_ref[...] += jnp.dot(w.astype(BF16), v2, preferred_element_type=F32)
        surv_ref[...] += total

    def band_body(jj, carry):
        band_block = band - 1 - jj
        fold(qi * band + band_block, band_block)
        return carry

    def earlier_body(jj, carry):
        fold(qi * band - 1 - 2 * jj, None)
        fold(qi * band - 2 - 2 * jj, None)
        return carry

    assert band % 2 == 0
    lax.fori_loop(0, band, band_body, 0)
    lax.fori_loop(0, qi * (band // 2), earlier_body, 0)
    o_ref[0] = acc_ref[...].astype(o_ref.dtype)


def _suffix_sum_matrix(blk):
    j = jnp.arange(blk)[:, None]
    s = jnp.arange(blk)[None, :]
    tri = (j >= s).astype(BF16)
    half = jnp.concatenate([tri, jnp.ones((blk, blk), BF16)], axis=1)
    return jnp.concatenate([half, half], axis=0)


def sb_attention_prompt(q, ktb, vb, bias):
    b, s, _ = q.shape
    blk = SB_BLOCK
    tq = SB_QUERY_TILE
    assert blk == PAGE_SIZE and s % tq == 0
    n_pairs = SB_HEADS // 2
    bias2 = jnp.repeat(bias.astype(F32).reshape(n_pairs, 2), blk, axis=1).reshape(n_pairs, 1, 2 * blk)
    return pl.pallas_call(
        _sb_prompt_kernel,
        grid=(b, n_pairs, s // tq),
        in_specs=[pl.BlockSpec((1, tq, LANES), lambda bi, hp, qi: (bi, qi, hp)),
                  pl.BlockSpec((s // blk, LANES, blk), lambda bi, hp, qi: (bi, hp, 0)),
                  pl.BlockSpec((1, s, LANES), lambda bi, hp, qi: (bi, 0, hp)),
                  pl.BlockSpec((1, 1, 2 * blk), lambda bi, hp, qi: (hp, 0, 0)),
                  pl.BlockSpec((2 * blk, 2 * blk), lambda bi, hp, qi: (0, 0))],
        out_specs=pl.BlockSpec((1, tq, LANES), lambda bi, hp, qi: (bi, qi, hp)),
        out_shape=jax.ShapeDtypeStruct((b, s, D_MODEL), BF16),
        scratch_shapes=[pltpu.VMEM((tq, 2 * blk), F32), pltpu.VMEM((tq, LANES), F32)],
        compiler_params=_params("parallel", "parallel", "arbitrary"),
        name="sb_attention_prompt",
    )(q, ktb, vb, bias2, _suffix_sum_matrix(blk))


def _sb_sample_kernel(n_new, pt_ref, qbd_ref, bias_ref, kn_ref, vn_ref, tri_ref, *rest):
    n_pg = PAGES_PER_STEP
    k_refs = rest[:n_pg]
    v_refs = rest[n_pg:2 * n_pg]
    o_ref, surv_ref, acc_ref = rest[2 * n_pg:]
    step = pl.program_id(1)
    qbd = qbd_ref[0]
    bias = bias_ref[...]
    pg = PAGE_SIZE
    rows = qbd.shape[0]

    def fold(kt, vt, visible):
        z = jnp.dot(qbd, kt, preferred_element_type=F32) + bias
        sp = _softplus(z)
        if visible is not None:
            sp = jnp.where(visible, sp, 0.0)
        hi, lo = _split2(sp)
        r = jnp.dot(jnp.concatenate([hi, lo], axis=1), tri_ref[...], preferred_element_type=F32)
        w = jnp.exp(z - r[:, :pg] - surv_ref[...])
        if visible is not None:
            w = jnp.where(visible, w, 0.0)
        acc_ref[...] += lax.dot_general(w.astype(BF16), vt, NT_DIMS, preferred_element_type=F32)
        surv_ref[...] += r[:, pg:]

    @pl.when(step == 0)
    def _():
        surv_ref[...] = jnp.zeros_like(surv_ref)
        acc_ref[...] = jnp.zeros_like(acc_ref)
        query = lax.broadcasted_iota(jnp.int32, (rows, pg), 0) // SB_HEADS
        key = lax.broadcasted_iota(jnp.int32, (rows, pg), 1)
        fold(kn_ref[0], vn_ref[0], (key < query) & (key < n_new))

    for i in range(n_pg):
        fold(k_refs[i][0].astype(BF16), v_refs[i][0].astype(BF16), None)

    @pl.when(step == pl.num_programs(1) - 1)
    def _():
        head_of_row = lax.broadcasted_iota(jnp.int32, (SB_HEADS, D_MODEL), 0)
        head_of_lane = lax.broadcasted_iota(jnp.int32, (SB_HEADS, D_MODEL), 1) // SB_HEAD_DIM
        out = []
        for t in range(o_ref.shape[1]):
            if t < n_new:
                blk = acc_ref[t * SB_HEADS:(t + 1) * SB_HEADS, :]
                out.append(jnp.sum(jnp.where(head_of_row == head_of_lane, blk, 0.0), axis=0, keepdims=True))
            else:
                out.append(jnp.zeros((1, D_MODEL), F32))
        o_ref[0] = jnp.concatenate(out, axis=0).astype(o_ref.dtype)


def sb_attention_sample(q, kt_new, vt_new, cache_kt, cache_vt, page_table, bias):
    b, t, _ = q.shape
    n_pages = page_table.shape[1]
    n_pg = PAGES_PER_STEP
    rows = t * SB_HEADS
    assert n_pages % n_pg == 0 and rows % 16 == 0 and t <= 8
    t_pad = 8
    q4 = q.reshape(b, t, SB_HEADS, SB_HEAD_DIM)
    eye = jnp.eye(SB_HEADS, dtype=BF16)
    qbd = jnp.einsum("bthd,hg->bthgd", q4, eye).reshape(b, rows, D_MODEL)
    bias_rows = jnp.tile(bias.astype(F32), t).reshape(rows, 1)
    new_pages = lambda a: jnp.pad(a.reshape(D_MODEL, b, t).transpose(1, 0, 2),
                                  ((0, 0), (0, 0), (0, PAGE_SIZE - t))).astype(BF16)

    def page_map(i):
        return lambda bi, si, pt: (pt[bi * n_pages + (n_pages - 1 - (si * n_pg + i))], 0, 0)

    per_b = lambda bi, si, pt: (bi, 0, 0)
    fixed = lambda bi, si, pt: (0, 0)
    page_specs = [pl.BlockSpec((1, D_MODEL, PAGE_SIZE), page_map(i)) for i in range(n_pg)]
    grid_spec = pltpu.PrefetchScalarGridSpec(
        num_scalar_prefetch=1,
        grid=(b, n_pages // n_pg),
        in_specs=[pl.BlockSpec((1, rows, D_MODEL), per_b),
                  pl.BlockSpec((rows, 1), fixed),
                  pl.BlockSpec((1, D_MODEL, PAGE_SIZE), per_b),
                  pl.BlockSpec((1, D_MODEL, PAGE_SIZE), per_b),
                  pl.BlockSpec((2 * PAGE_SIZE, 2 * PAGE_SIZE), fixed)] + page_specs + page_specs,
        out_specs=pl.BlockSpec((1, t_pad, D_MODEL), per_b),
        scratch_shapes=[pltpu.VMEM((rows, PAGE_SIZE), F32), pltpu.VMEM((rows, D_MODEL), F32)],
    )
    o = pl.pallas_call(
        functools.partial(_sb_sample_kernel, t),
        grid_spec=grid_spec,
        out_shape=jax.ShapeDtypeStruct((b, t_pad, D_MODEL), BF16),
        compiler_params=_params("parallel", "arbitrary"),
        name="sb_attention_sample",
    )(page_table.reshape(-1).astype(jnp.int32), qbd, bias_rows, new_pages(kt_new), new_pages(vt_new),
      _suffix_sum_matrix(PAGE_SIZE), *([cache_kt] * n_pg), *([cache_vt] * n_pg))
    return o[:, :t]


def _ssd_kernel(valid_len, has_init,
                x_ref, bm_ref, cm_ref, z_ref, dt_ref,
                cwx_ref, cwb_ref, cwc_ref, cbx_ref, cbb_ref, cbc_ref,
                px_ref, pb_ref, pc_ref, init_ref,
                dtb_ref, a_ref, aexp_ref, dexp_ref, nw_ref, expand_ref,
                y_ref, fin_ref,
                ext_x, ext_b, ext_c, state_ref):
    lc = x_ref.shape[1]
    c = pl.program_id(2)
    tail = 8

    @pl.when(c == 0)
    def _():
        ext_x[0:tail, :] = px_ref[0]
        ext_b[0:tail, :] = pb_ref[0]
        ext_c[0:tail, :] = pc_ref[0]
        if has_init:
            n = SSM_HEADS_PER_GROUP * SSM_HEAD_DIM
            state_ref[...] = init_ref[0].reshape(n, SSM_D_STATE).T
        else:
            state_ref[...] = jnp.zeros_like(state_ref)

    def conv_silu(cur_ref, ext, w_ref, b_ref):
        ext[tail:tail + lc, :] = cur_ref[0]
        acc = b_ref[...] + w_ref[SSM_CONV_W - 1:SSM_CONV_W, :] * ext[tail:tail + lc, :]
        for back in range(1, SSM_CONV_W):
            tap = SSM_CONV_W - 1 - back
            acc = acc + w_ref[tap:tap + 1, :] * ext[tail - back:tail - back + lc, :]
        ext[0:tail, :] = ext[lc:lc + tail, :]
        return acc * _sigmoid(acc)

    xs = conv_silu(x_ref, ext_x, cwx_ref, cbx_ref)
    bm = conv_silu(bm_ref, ext_b, cwb_ref, cbb_ref)
    cm = conv_silu(cm_ref, ext_c, cwc_ref, cbc_ref)

    dt = _softplus(dt_ref[0] + dtb_ref[0])
    if valid_len < lc:
        dt = jnp.where(lax.broadcasted_iota(jnp.int32, dt.shape, 0) < valid_len, dt, 0.0)
    t_idx = lax.broadcasted_iota(jnp.int32, (lc, lc), 0)
    s_idx = lax.broadcasted_iota(jnp.int32, (lc, lc), 1)
    causal = s_idx <= t_idx
    lower = causal.astype(BF16)

    def cumsum_rows(v):
        parts = _split3(v)
        return sum(jnp.dot(lower, p, preferred_element_type=F32) for p in parts)

    a_cum = cumsum_rows(dt * a_ref[0])
    a_cum_t = a_cum.T
    expand = expand_ref[...]
    dt_wide = sum(jnp.dot(p, expand, preferred_element_type=F32) for p in _split3(dt))
    a_cum_wide = cumsum_rows(dt_wide * aexp_ref[...])
    a_last_wide = a_cum_wide[lc - 1:lc, :]

    bm_t = bm.T.astype(BF16)
    cm_b = cm.astype(BF16)
    cb = jnp.dot(cm_b, bm_t, preferred_element_type=F32)
    xdt = xs * dt_wide
    xdt_b = xdt.astype(BF16)
    first_half = lax.broadcasted_iota(jnp.int32, (lc, LANES), 1) < SSM_HEAD_DIM
    y_parts = []
    for pair in range(SSM_HEADS_PER_GROUP // 2):
        halves = []
        for k in (2 * pair, 2 * pair + 1):
            seg = a_cum[:, k:k + 1] - a_cum_t[k:k + 1, :]
            m = (cb * jnp.exp(jnp.where(causal, seg, -jnp.inf))).astype(BF16)
            halves.append(jnp.dot(m, xdt_b[:, pair * LANES:(pair + 1) * LANES], preferred_element_type=F32))
        y_parts.append(jnp.where(first_half, halves[0], halves[1]))
    y = jnp.concatenate(y_parts, axis=1)

    state = state_ref[...]
    y = y + jnp.dot(cm_b, state.astype(BF16), preferred_element_type=F32) * jnp.exp(a_cum_wide)
    y = y + dexp_ref[...] * xs
    y = y * (z_ref[0] * _sigmoid(z_ref[0]))
    y = y * lax.rsqrt(jnp.mean(y * y, axis=-1, keepdims=True) + RMS_EPS) * nw_ref[...]
    y_ref[0] = y.astype(y_ref.dtype)

    x_to_end = (xdt * jnp.exp(a_last_wide - a_cum_wide)).astype(BF16)
    new_state = state * jnp.exp(a_last_wide) + jnp.dot(bm_t, x_to_end, preferred_element_type=F32)
    state_ref[...] = new_state

    @pl.when(c == pl.num_programs(2) - 1)
    def _():
        fin_ref[0] = new_state.T.reshape(SSM_HEADS_PER_GROUP, SSM_HEAD_DIM, SSM_D_STATE)


def ssd_mixer(z, xbc, dt, conv_prev, init_state, valid_len, p):
    b, l, _ = z.shape
    lc = SSM_CHUNK
    assert l % lc == 0
    n_x = SSM_D_INNER // SSM_GROUP_DIM
    gd, ns = SSM_GROUP_DIM, SSM_D_STATE
    b_off = SSM_D_INNER // ns
    c_off = b_off + SSM_GROUPS
    has_init = init_state is not None
    if not has_init:
        init_state = jnp.zeros((1, SSM_HEADS, SSM_HEAD_DIM, ns), F32)

    seq = lambda off: (lambda bi, g, c: (bi, c, off + g))
    par = lambda off: (lambda bi, g, c: (0, off + g))
    prev = lambda off: (lambda bi, g, c: (bi, 0, off + g))
    init_map = (lambda bi, g, c: (bi, g, 0, 0)) if has_init else (lambda bi, g, c: (0, g, 0, 0))
    hg = SSM_HEADS_PER_GROUP
    in_specs = [
        pl.BlockSpec((1, lc, gd), seq(0)), pl.BlockSpec((1, lc, ns), seq(b_off)), pl.BlockSpec((1, lc, ns), seq(c_off)),
        pl.BlockSpec((1, lc, gd), seq(0)), pl.BlockSpec((1, lc, LANES), seq(0)),
        pl.BlockSpec((SSM_CONV_W, gd), par(0)), pl.BlockSpec((SSM_CONV_W, ns), par(b_off)),
        pl.BlockSpec((SSM_CONV_W, ns), par(c_off)),
        pl.BlockSpec((1, gd), par(0)), pl.BlockSpec((1, ns), par(b_off)), pl.BlockSpec((1, ns), par(c_off)),
        pl.BlockSpec((1, 8, gd), prev(0)), pl.BlockSpec((1, 8, ns), prev(b_off)), pl.BlockSpec((1, 8, ns), prev(c_off)),
        pl.BlockSpec((1, hg, SSM_HEAD_DIM, ns), init_map),
        pl.BlockSpec((1, 1, LANES), lambda bi, g, c: (g, 0, 0)),
        pl.BlockSpec((1, 1, LANES), lambda bi, g, c: (g, 0, 0)),
        pl.BlockSpec((1, gd), par(0)), pl.BlockSpec((1, gd), par(0)), pl.BlockSpec((1, gd), par(0)),
        pl.BlockSpec((LANES, gd), lambda bi, g, c: (0, 0)),
    ]
    out_specs = [pl.BlockSpec((1, lc, gd), seq(0)),
                 pl.BlockSpec((1, hg, SSM_HEAD_DIM, ns), lambda bi, g, c: (bi, g, 0, 0))]
    return pl.pallas_call(
        functools.partial(_ssd_kernel, valid_len, has_init),
        grid=(b, SSM_GROUPS, l // lc),
        in_specs=in_specs,
        out_specs=out_specs,
        out_shape=[jax.ShapeDtypeStruct((b, l, SSM_D_INNER), BF16),
                   jax.ShapeDtypeStruct((b, SSM_HEADS, SSM_HEAD_DIM, ns), F32)],
        scratch_shapes=[pltpu.VMEM((lc + 8, gd), F32), pltpu.VMEM((lc + 8, ns), F32),
                        pltpu.VMEM((lc + 8, ns), F32), pltpu.VMEM((ns, gd), F32)],
        compiler_params=_params("parallel", "parallel", "arbitrary"),
        name="ssd_mixer",
    )(xbc, xbc, xbc, z, dt,
      p["conv_w"], p["conv_w"], p["conv_w"], p["conv_b"], p["conv_b"], p["conv_b"],
      conv_prev, conv_prev, conv_prev, init_state,
      p["dt_bias"], p["a_neg"], p["a_wide"], p["d_wide"], p["norm_w"], p["expand"])


def ssm_params(w_in, conv_w, conv_b, dt_bias, a_log, d_skip, norm_w, w_out):
    hg = SSM_HEADS_PER_GROUP
    w_dt = w_in[:, SSM_D_INNER + SSM_CONV_DIM:].reshape(D_MODEL, SSM_GROUPS, hg)
    w_dt = jnp.pad(w_dt, ((0, 0), (0, 0), (0, LANES - hg))).reshape(D_MODEL, SSM_DT_COLS)
    lane_pad = lambda v: jnp.pad(v.astype(F32).reshape(SSM_GROUPS, 1, hg), ((0, 0), (0, 0), (0, LANES - hg)))
    a_neg = -jnp.exp(a_log.astype(F32))
    head_of_chan = jnp.arange(SSM_GROUP_DIM) // SSM_HEAD_DIM
    expand = (jnp.arange(LANES)[:, None] == head_of_chan[None, :]).astype(BF16)
    return {
        "w_z": w_in[:, :SSM_D_INNER].astype(BF16),
        "w_xbc": w_in[:, SSM_D_INNER:SSM_D_INNER + SSM_CONV_DIM].astype(BF16),
        "w_dt": w_dt.astype(BF16),
        "conv_w": conv_w.astype(F32), "conv_b": conv_b.astype(F32).reshape(1, SSM_CONV_DIM),
        "dt_bias": lane_pad(dt_bias), "a_neg": lane_pad(a_neg),
        "a_wide": jnp.repeat(a_neg, SSM_HEAD_DIM).reshape(1, SSM_D_INNER),
        "d_wide": jnp.repeat(d_skip.astype(F32), SSM_HEAD_DIM).reshape(1, SSM_D_INNER),
        "norm_w": norm_w.astype(F32).reshape(1, SSM_D_INNER),
        "expand": expand,
        "w_out": w_out.astype(BF16),
    }


def _router_kernel(h_ref, g_ref, w_ref, b_ref, before_ref, xn_ref, meta_ref, cnt_ref, carry_ref):
    step = pl.program_id(0)
    tm = h_ref.shape[0]

    @pl.when(step == 0)
    def _():
        carry_ref[...] = jnp.zeros_like(carry_ref)

    xn = _rms_norm(h_ref[...], g_ref[...])
    xn_ref[...] = xn
    logits = lax.dot_general(w_ref[...], xn.astype(BF16), NT_DIMS, preferred_element_type=F32) + b_ref[...]

    sub8 = lax.broadcasted_iota(jnp.int32, (8, tm), 0)
    neg_inf = jnp.float32(-jnp.inf)

    def first_argmax(v, vmax):
        return jnp.min(jnp.where(v == vmax, sub8, 8), axis=0, keepdims=True)

    grp = jnp.where(sub8 < MOE_GROUPS, logits[0:8], neg_inf)
    g_max = jnp.max(grp, axis=0, keepdims=True)
    g_idx = first_argmax(grp, g_max)
    g_prob = 1.0 / jnp.sum(jnp.exp(grp - g_max), axis=0, keepdims=True)

    in_group = logits[8:16]
    for g in range(1, MOE_GROUPS):
        in_group = jnp.where(g_idx == g, logits[8 + 8 * g:16 + 8 * g], in_group)
    m1 = jnp.max(in_group, axis=0, keepdims=True)
    i1 = first_argmax(in_group, m1)
    rest = jnp.where(sub8 == i1, neg_inf, in_group)
    m2 = jnp.max(rest, axis=0, keepdims=True)
    i2 = first_argmax(rest, m2)
    p2 = jnp.exp(m2 - m1)
    w1 = g_prob / (1.0 + p2)
    w2 = g_prob * p2 / (1.0 + p2)
    e1 = g_idx * MOE_EXPERTS_PER_GROUP + i1
    e2 = g_idx * MOE_EXPERTS_PER_GROUP + i2

    sub_e = lax.broadcasted_iota(jnp.int32, (MOE_EXPERTS, tm), 0)
    hot1 = sub_e == e1
    hot2 = sub_e == e2
    hot = jnp.concatenate([hot1, hot2], axis=0).astype(BF16)
    earlier = jnp.dot(hot, before_ref[...], preferred_element_type=F32)
    hot1f = hot1.astype(F32)
    hot2f = hot2.astype(F32)
    cnt1 = jnp.sum(hot1f, axis=1, keepdims=True)
    cnt2 = jnp.sum(hot2f, axis=1, keepdims=True)
    carry = carry_ref[...]
    r1 = jnp.sum(hot1f * (earlier[:MOE_EXPERTS] + carry), axis=0, keepdims=True)
    r2 = jnp.sum(hot2f * (earlier[MOE_EXPERTS:] + carry + cnt1), axis=0, keepdims=True)
    carry = carry + cnt1 + cnt2
    carry_ref[...] = carry
    cnt_ref[...] = carry
    zero = jnp.zeros((1, tm), F32)
    meta_ref[...] = jnp.concatenate(
        [e1.astype(F32), e2.astype(F32), w1, w2, r1, r2, zero, zero], axis=0)


def moe_route(h, gain, r):
    t = h.shape[0]
    tm = _row_tile(t, 512)
    i = jnp.arange(tm)
    before = (i[:, None] < i[None, :]).astype(BF16)
    fixed = lambda s: (0, 0)
    return pl.pallas_call(
        _router_kernel,
        grid=(t // tm,),
        in_specs=[pl.BlockSpec((tm, D_MODEL), lambda s: (s, 0)),
                  pl.BlockSpec((1, D_MODEL), fixed),
                  pl.BlockSpec((ROUTE_ROWS, D_MODEL), fixed),
                  pl.BlockSpec((ROUTE_ROWS, 1), fixed),
                  pl.BlockSpec((tm, tm), fixed)],
        out_specs=[pl.BlockSpec((tm, D_MODEL), lambda s: (s, 0)),
                   pl.BlockSpec((8, tm), lambda s: (0, s)),
                   pl.BlockSpec((MOE_EXPERTS, 1), fixed)],
        out_shape=[jax.ShapeDtypeStruct((t, D_MODEL), F32),
                   jax.ShapeDtypeStruct((8, t), F32),
                   jax.ShapeDtypeStruct((MOE_EXPERTS, 1), F32)],
        scratch_shapes=[pltpu.VMEM((MOE_EXPERTS, 1), F32)],
        compiler_params=_params("arbitrary"),
        name="moe_route",
    )(h, gain, r["w_route"], r["bias"], before)


def _scatter_rows_kernel(pos_ref, x_ref, rows_in_ref, rows_ref, sem):
    del rows_in_ref
    tm = x_ref.shape[0]

    def copy(k, t):
        return pltpu.make_async_copy(x_ref.at[pl.ds(t, 1), :], rows_ref.at[pl.ds(pos_ref[k, t], 1), :], sem)

    def start(t, carry):
        copy(0, t).start()
        copy(1, t).start()
        return carry

    def wait(t, carry):
        copy(0, t).wait()
        copy(1, t).wait()
        return carry

    lax.fori_loop(0, tm, start, 0, unroll=ROW_DMA_UNROLL)
    lax.fori_loop(0, tm, wait, 0, unroll=ROW_DMA_UNROLL)


def moe_scatter_rows(xn, pos, n_rows):
    t = xn.shape[0]
    tm = _row_tile(t, 512)
    rows0 = jnp.zeros((n_rows, D_MODEL), F32)
    return pl.pallas_call(
        _scatter_rows_kernel,
        grid=(t // tm,),
        in_specs=[pl.BlockSpec((2, tm), lambda s: (0, s), memory_space=pltpu.SMEM),
                  pl.BlockSpec((tm, D_MODEL), lambda s: (s, 0)),
                  pl.BlockSpec(memory_space=pl.ANY)],
        out_specs=pl.BlockSpec(memory_space=pl.ANY),
        out_shape=jax.ShapeDtypeStruct((n_rows, D_MODEL), F32),
        scratch_shapes=[pltpu.SemaphoreType.DMA(())],
        input_output_aliases={2: 0},
        compiler_params=_params("arbitrary"),
        name="moe_scatter_rows",
    )(pos, xn, rows0)


def _expert_mlp_kernel(be_ref, nb_ref, x_ref, wgu_ref, wd_ref, o_ref):
    i = pl.program_id(0)

    @pl.when(i < nb_ref[0])
    def _():
        x = x_ref[...].astype(BF16)
        gate = jnp.dot(x, wgu_ref[0, :, :MOE_D_FF], preferred_element_type=F32)
        up = jnp.dot(x, wgu_ref[0, :, MOE_D_FF:], preferred_element_type=F32)
        act = (gate * _sigmoid(gate) * up).astype(BF16)
        o_ref[...] = jnp.dot(act, wd_ref[0], preferred_element_type=F32)

    @pl.when(i >= nb_ref[0])
    def _():
        o_ref[...] = jnp.zeros_like(o_ref)


def moe_expert_mlp(rows, block_expert, n_used, w_gu, w_down):
    n_rows = rows.shape[0]
    br = MOE_BLOCK_ROWS
    grid_spec = pltpu.PrefetchScalarGridSpec(
        num_scalar_prefetch=2,
        grid=(n_rows // br,),
        in_specs=[pl.BlockSpec((br, D_MODEL), lambda i, be, nb: (i, 0)),
                  pl.BlockSpec((1, D_MODEL, 2 * MOE_D_FF), lambda i, be, nb: (be[i], 0, 0)),
                  pl.BlockSpec((1, MOE_D_FF, D_MODEL), lambda i, be, nb: (be[i], 0, 0))],
        out_specs=pl.BlockSpec((br, D_MODEL), lambda i, be, nb: (i, 0)),
    )
    return pl.pallas_call(
        _expert_mlp_kernel,
        grid_spec=grid_spec,
        out_shape=jax.ShapeDtypeStruct((n_rows, D_MODEL), F32),
        compiler_params=_params("arbitrary"),
        name="moe_expert_mlp",
    )(block_expert, n_used, rows, w_gu, w_down)


def _combine_kernel(final_norm, pos_ref, h_ref, w_ref, rows_ref, g_ref, o_ref, y_ref, buf1, buf2, sem):
    tm = h_ref.shape[0]

    def copies(t):
        return (pltpu.make_async_copy(rows_ref.at[pl.ds(pos_ref[0, t], 1), :], buf1.at[pl.ds(t, 1), :], sem),
                pltpu.make_async_copy(rows_ref.at[pl.ds(pos_ref[1, t], 1), :], buf2.at[pl.ds(t, 1), :], sem))

    def start(t, carry):
        c1, c2 = copies(t)
        c1.start()
        c2.start()
        return carry

    def wait(t, carry):
        c1, c2 = copies(t)
        c1.wait()
        c2.wait()
        return carry

    lax.fori_loop(0, tm, start, 0, unroll=ROW_DMA_UNROLL)
    lax.fori_loop(0, tm, wait, 0, unroll=ROW_DMA_UNROLL)
    w = w_ref[...]
    out = h_ref[...] + (w[:, 0:1] * buf1[...] + w[:, 1:2] * buf2[...])
    o_ref[...] = out
    if final_norm:
        y_ref[...] = _rms_norm(out, g_ref[...])
    else:
        y_ref[...] = jnp.zeros_like(y_ref)


def moe_combine(h, pos, w_cols, out_rows, final_gain):
    t = h.shape[0]
    tm = _row_tile(t, 256)
    final_norm = final_gain is not None
    gain = final_gain if final_norm else jnp.ones((1, D_MODEL), F32)
    y_rows = tm if final_norm else 8
    y_shape = (t, D_MODEL) if final_norm else (8 * (t // tm), D_MODEL)
    out, y = pl.pallas_call(
        functools.partial(_combine_kernel, final_norm),
        grid=(t // tm,),
        in_specs=[pl.BlockSpec((2, tm), lambda s: (0, s), memory_space=pltpu.SMEM),
                  pl.BlockSpec((tm, D_MODEL), lambda s: (s, 0)),
                  pl.BlockSpec((tm, 2), lambda s: (s, 0)),
                  pl.BlockSpec(memory_space=pl.ANY),
                  pl.BlockSpec((1, D_MODEL), lambda s: (0, 0))],
        out_specs=[pl.BlockSpec((tm, D_MODEL), lambda s: (s, 0)),
                   pl.BlockSpec((y_rows, D_MODEL), lambda s: (s, 0))],
        out_shape=[jax.ShapeDtypeStruct((t, D_MODEL), F32), jax.ShapeDtypeStruct(y_shape, F32)],
        scratch_shapes=[pltpu.VMEM((tm, D_MODEL), F32), pltpu.VMEM((tm, D_MODEL), F32),
                        pltpu.SemaphoreType.DMA(())],
        compiler_params=_params("arbitrary"),
        name="moe_combine",
    )(pos, h, w_cols, out_rows, gain)
    return out, (y if final_norm else None)


def moe_params(w_group, b_group, w_route, b_route, w_gu, w_down):
    w_rt = jnp.zeros((ROUTE_ROWS, D_MODEL), F32)
    w_rt = w_rt.at[0:MOE_GROUPS].set(w_group.astype(F32).T)
    w_rt = w_rt.at[8:8 + MOE_EXPERTS].set(w_route.astype(F32).T)
    bias = jnp.zeros((ROUTE_ROWS,), F32)
    bias = bias.at[0:MOE_GROUPS].set(b_group.astype(F32)).at[8:8 + MOE_EXPERTS].set(b_route.astype(F32))
    return {"w_route": w_rt.astype(BF16), "bias": bias.reshape(ROUTE_ROWS, 1),
            "w_gu": w_gu.astype(BF16), "w_down": w_down.astype(BF16)}


def hier_moe_residual(h, gain, r, final_gain=None):
    t = h.shape[0]
    br = MOE_BLOCK_ROWS
    xn, meta, counts = moe_route(h, gain, r)
    counts = counts[:, 0].astype(jnp.int32)
    padded = (counts + br - 1) // br * br
    padded_end = jnp.cumsum(padded)
    padded_start = padded_end - padded
    expert = meta[0:2].astype(jnp.int32)
    hot = expert[..., None] == jnp.arange(MOE_EXPERTS, dtype=jnp.int32)
    pos = jnp.sum(jnp.where(hot, padded_start, 0), axis=-1) + meta[4:6].astype(jnp.int32)
    n_blocks = -(-(t * MOE_TOP_K) // br) + MOE_EXPERTS
    block_start = jnp.arange(n_blocks, dtype=jnp.int32) * br
    block_expert = jnp.minimum(jnp.sum(padded_end[None, :] <= block_start[:, None], axis=1), MOE_EXPERTS - 1)
    n_used = (padded_end[-1] // br).reshape(1)
    rows = moe_scatter_rows(xn, pos, n_blocks * br)
    out_rows = moe_expert_mlp(rows, block_expert.astype(jnp.int32), n_used.astype(jnp.int32), r["w_gu"], r["w_down"])
    return moe_combine(h, pos, meta[2:4].T, out_rows, final_gain)


def kernel(x_prompt, x_sample, cache_k, cache_v, page_table, state_ssm, state_conv, norm_mix, norm_ffn, norm_final,
           sb_w_qkv, sb_bias, sb_w_o, ssm_w_in, ssm_conv_w, ssm_conv_b, ssm_dt_bias, ssm_a_log, ssm_d, ssm_norm_w,
           ssm_w_out, moe_w_group, moe_b_group, moe_w_route, moe_b_route, moe_w_gu, moe_w_down):
    bp, sp, _ = x_prompt.shape
    bs, ts, _ = x_sample.shape
    depth = norm_mix.shape[0]
    hp = x_prompt.reshape(bp * sp, D_MODEL)
    hs = x_sample.reshape(bs * ts, D_MODEL)
    n_phys = cache_k.shape[1]
    gain = lambda g: g.astype(F32).reshape(1, D_MODEL)
    outs = {name: [] for name in ("k_p", "v_p", "k_s", "v_s", "ssm_p", "conv_p", "ssm_s", "conv_s")}
    yp = ys = None
    for layer in range(depth):
        slot = layer // N_MIXERS
        g_mix = gain(norm_mix[layer])
        if layer % N_MIXERS == 0:
            w = sb_w_qkv[slot]
            w_qv = jnp.concatenate([w[:, :D_MODEL], w[:, 2 * D_MODEL:]], axis=1).astype(BF16)
            w_kv_t = w[:, D_MODEL:].T.astype(BF16)
            w_o = sb_w_o[slot].astype(BF16)
            q, kt, vt, ktb, vb = qkv_project(hp, g_mix, w_qv, w_kv_t)
            shape3 = (bp, sp, D_MODEL)
            o = sb_attention_prompt(q.reshape(shape3), ktb, vb.reshape(shape3), sb_bias[slot])
            hp = out_project_residual(hp, o.reshape(bp * sp, D_MODEL), w_o)
            pages = lambda a: a.reshape(bp, sp // PAGE_SIZE, SB_HEADS, SB_HEAD_DIM, PAGE_SIZE).transpose(0, 1, 4, 2, 3)
            outs["k_p"].append(pages(kt))
            outs["v_p"].append(pages(vt))

            q, kt, vt, _, _ = qkv_project(hs, g_mix, w_qv, w_kv_t)
            kt, vt = kt[0], vt[0]
            cache_t = lambda c: c[slot].transpose(0, 2, 3, 1).reshape(n_phys, D_MODEL, PAGE_SIZE)
            o = sb_attention_sample(q.reshape(bs, ts, D_MODEL), kt, vt, cache_t(cache_k), cache_t(cache_v),
                                    page_table, sb_bias[slot])
            hs = out_project_residual(hs, o.reshape(bs * ts, D_MODEL), w_o)
            rows = lambda a: a.reshape(SB_HEADS, SB_HEAD_DIM, bs, ts).transpose(2, 3, 0, 1)
            outs["k_s"].append(rows(kt))
            outs["v_s"].append(rows(vt))
        else:
            p = ssm_params(ssm_w_in[slot], ssm_conv_w[slot], ssm_conv_b[slot], ssm_dt_bias[slot], ssm_a_log[slot],
                           ssm_d[slot], ssm_norm_w[slot], ssm_w_out[slot])
            z, xbc, dt = ssm_in_project(hp, g_mix, p["w_z"], p["w_xbc"], p["w_dt"])
            xbc3 = xbc.reshape(bp, sp, SSM_CONV_DIM)
            y, fin = ssd_mixer(z.reshape(bp, sp, SSM_D_INNER), xbc3, dt.reshape(bp, sp, SSM_DT_COLS),
                               jnp.zeros((bp, 8, SSM_CONV_DIM), F32), None, SSM_CHUNK, p)
            hp = out_project_residual(hp, y.reshape(bp * sp, SSM_D_INNER), p["w_out"])
            outs["ssm_p"].append(fin)
            outs["conv_p"].append(xbc3[:, sp - (SSM_CONV_W - 1):])

            z, xbc, dt = ssm_in_project(hs, g_mix, p["w_z"], p["w_xbc"], p["w_dt"])
            pad_seq = lambda a: jnp.pad(a.reshape(bs, ts, -1), ((0, 0), (0, SSM_CHUNK - ts), (0, 0)))
            conv_prev = jnp.pad(state_conv[slot].astype(F32), ((0, 0), (8 - (SSM_CONV_W - 1), 0), (0, 0)))
            y, fin = ssd_mixer(pad_seq(z), pad_seq(xbc), pad_seq(dt), conv_prev, state_ssm[slot].astype(F32), ts, p)
            hs = out_project_residual(hs, y[:, :ts].reshape(bs * ts, SSM_D_INNER), p["w_out"])
            xbc_ext = jnp.concatenate([state_conv[slot].astype(F32), xbc.reshape(bs, ts, SSM_CONV_DIM)], axis=1)
            outs["ssm_s"].append(fin)
            outs["conv_s"].append(xbc_ext[:, ts:])
        r = moe_params(moe_w_group[layer], moe_b_group[layer], moe_w_route[layer], moe_b_route[layer],
                       moe_w_gu[layer], moe_w_down[layer])
        final_gain = gain(norm_final) if layer == depth - 1 else None
        hp, yp = hier_moe_residual(hp, gain(norm_ffn[layer]), r, final_gain)
        hs, ys = hier_moe_residual(hs, gain(norm_ffn[layer]), r, final_gain)
    return (yp.reshape(bp, sp, D_MODEL), ys.reshape(bs, ts, D_MODEL),
            jnp.stack(outs["k_p"]), jnp.stack(outs["v_p"]), jnp.stack(outs["k_s"]), jnp.stack(outs["v_s"]),
            jnp.stack(outs["ssm_p"]), jnp.stack(outs["conv_p"]), jnp.stack(outs["ssm_s"]), jnp.stack(outs["conv_s"]))
```

```python
import functools
import math

import jax
import jax.numpy as jnp
from jax import lax
from jax.experimental import pallas as pl
from jax.experimental.pallas import tpu as pltpu

F32 = jnp.float32
BF16 = jnp.bfloat16

D_MODEL = 1024
RMS_EPS = 1e-6
N_MIXERS = 2

SB_HEADS = 16
SB_HEAD_DIM = D_MODEL // SB_HEADS
SB_SCALE = 1.0 / math.sqrt(SB_HEAD_DIM)
SB_BLOCK = 128
SB_QUERY_TILE = 512
PAGE_SIZE = 128
PAGES_PER_STEP = 8

SSM_D_INNER = 2 * D_MODEL
SSM_HEAD_DIM = 64
SSM_HEADS = SSM_D_INNER // SSM_HEAD_DIM
SSM_GROUPS = 4
SSM_HEADS_PER_GROUP = SSM_HEADS // SSM_GROUPS
SSM_GROUP_DIM = SSM_D_INNER // SSM_GROUPS
SSM_D_STATE = 128
SSM_CONV_W = 4
SSM_CONV_DIM = SSM_D_INNER + 2 * SSM_GROUPS * SSM_D_STATE
SSM_CHUNK = 128

MOE_GROUPS = 4
MOE_EXPERTS_PER_GROUP = 8
MOE_EXPERTS = MOE_GROUPS * MOE_EXPERTS_PER_GROUP
MOE_TOP_K = 2
MOE_D_FF = D_MODEL // 2
MOE_BLOCK_ROWS = 256
ROUTE_ROWS = 48
ROW_DMA_UNROLL = 8

LANES = 128
VMEM_LIMIT_BYTES = 48 * 1024 * 1024

NT_DIMS = (((1,), (1,)), ((), ()))


def _params(*semantics):
    return pltpu.CompilerParams(dimension_semantics=semantics, vmem_limit_bytes=VMEM_LIMIT_BYTES)


def _rms_norm(x, gain):
    return x * lax.rsqrt(jnp.mean(x * x, axis=-1, keepdims=True) + RMS_EPS) * gain


def _split2(x):
    hi = x.astype(BF16)
    lo = (x - hi.astype(F32)).astype(BF16)
    return hi, lo


def _split3(x):
    hi = x.astype(BF16)
    r = x - hi.astype(F32)
    mid = r.astype(BF16)
    lo = (r - mid.astype(F32)).astype(BF16)
    return hi, mid, lo


def _softplus(z):
    return jnp.maximum(z, 0.0) + jnp.log(1.0 + jnp.exp(-jnp.abs(z)))


LOG2E = 1.4426950408889634


def _softplus_log2(z2):
    return jnp.maximum(z2, 0.0) + jnp.log2(1.0 + jnp.exp2(-jnp.abs(z2)))


def _sigmoid(x):
    return 1.0 / (1.0 + jnp.exp(-x))


def _row_tile(t, want):
    return want if t % want == 0 else t


def _qkv_kernel(x_ref, g_ref, wqv_ref, wkvt_ref, q_ref, kt_ref, vt_ref, ktb_ref, vb_ref):
    xn = _rms_norm(x_ref[...], g_ref[...]).astype(BF16)
    q = jnp.dot(xn, wqv_ref[:, 0:D_MODEL], preferred_element_type=F32)
    v = jnp.dot(xn, wqv_ref[:, D_MODEL:2 * D_MODEL], preferred_element_type=F32)
    kt = lax.dot_general(wkvt_ref[0:D_MODEL, :], xn, NT_DIMS, preferred_element_type=F32)
    vt = lax.dot_general(wkvt_ref[D_MODEL:2 * D_MODEL, :], xn, NT_DIMS, preferred_element_type=F32)
    q_ref[...] = (q * SB_SCALE).astype(BF16)
    vb_ref[...] = v.astype(BF16)
    for p in range(kt_ref.shape[0]):
        cols = slice(p * PAGE_SIZE, (p + 1) * PAGE_SIZE)
        kt_ref[p] = kt[:, cols]
        vt_ref[p] = vt[:, cols]
        ktb_ref[p] = kt[:, cols].astype(BF16)


def qkv_project(h, gain, w_qv, w_kv_t):
    t = h.shape[0]
    tm = _row_tile(t, 256)
    pages = tm // PAGE_SIZE
    row = lambda i: (i, 0)
    fixed = lambda i: (0, 0)
    page = lambda i: (i, 0, 0)
    return pl.pallas_call(
        _qkv_kernel,
        grid=(t // tm,),
        in_specs=[pl.BlockSpec((tm, D_MODEL), row),
                  pl.BlockSpec((1, D_MODEL), fixed),
                  pl.BlockSpec((D_MODEL, 2 * D_MODEL), fixed),
                  pl.BlockSpec((2 * D_MODEL, D_MODEL), fixed)],
        out_specs=[pl.BlockSpec((tm, D_MODEL), row),
                   pl.BlockSpec((pages, D_MODEL, PAGE_SIZE), page),
                   pl.BlockSpec((pages, D_MODEL, PAGE_SIZE), page),
                   pl.BlockSpec((pages, D_MODEL, PAGE_SIZE), page),
                   pl.BlockSpec((tm, D_MODEL), row)],
        out_shape=[jax.ShapeDtypeStruct((t, D_MODEL), BF16),
                   jax.ShapeDtypeStruct((t // PAGE_SIZE, D_MODEL, PAGE_SIZE), F32),
                   jax.ShapeDtypeStruct((t // PAGE_SIZE, D_MODEL, PAGE_SIZE), F32),
                   jax.ShapeDtypeStruct((t // PAGE_SIZE, D_MODEL, PAGE_SIZE), BF16),
                   jax.ShapeDtypeStruct((t, D_MODEL), BF16)],
        compiler_params=_params("parallel"),
        name="qkv_project",
    )(h, gain, w_qv, w_kv_t)


SSM_DT_COLS = SSM_GROUPS * LANES


def _ssm_in_kernel(x_ref, g_ref, wz_ref, wx_ref, wd_ref, z_ref, xbc_ref, dt_ref):
    xn = _rms_norm(x_ref[...], g_ref[...]).astype(BF16)
    z_ref[...] = jnp.dot(xn, wz_ref[...], preferred_element_type=F32)
    xbc_ref[...] = jnp.dot(xn, wx_ref[...], preferred_element_type=F32)
    dt_ref[...] = jnp.dot(xn, wd_ref[...], preferred_element_type=F32)


def ssm_in_project(h, gain, w_z, w_xbc, w_dt):
    t = h.shape[0]
    tm = _row_tile(t, 256)
    row = lambda i: (i, 0)
    fixed = lambda i: (0, 0)
    return pl.pallas_call(
        _ssm_in_kernel,
        grid=(t // tm,),
        in_specs=[pl.BlockSpec((tm, D_MODEL), row),
                  pl.BlockSpec((1, D_MODEL), fixed),
                  pl.BlockSpec((D_MODEL, SSM_D_INNER), fixed),
                  pl.BlockSpec((D_MODEL, SSM_CONV_DIM), fixed),
                  pl.BlockSpec((D_MODEL, SSM_DT_COLS), fixed)],
        out_specs=[pl.BlockSpec((tm, SSM_D_INNER), row),
                   pl.BlockSpec((tm, SSM_CONV_DIM), row),
                   pl.BlockSpec((tm, SSM_DT_COLS), row)],
        out_shape=[jax.ShapeDtypeStruct((t, SSM_D_INNER), F32),
                   jax.ShapeDtypeStruct((t, SSM_CONV_DIM), F32),
                   jax.ShapeDtypeStruct((t, SSM_DT_COLS), F32)],
        compiler_params=_params("parallel"),
        name="ssm_in_project",
    )(h, gain, w_z, w_xbc, w_dt)


def _out_proj_kernel(h_ref, a_ref, w_ref, o_ref):
    o_ref[...] = h_ref[...] + jnp.dot(a_ref[...], w_ref[...], preferred_element_type=F32)


def out_project_residual(h, a_bf16, w_bf16):
    t, k = a_bf16.shape
    tm = _row_tile(t, 512)
    row = lambda i: (i, 0)
    return pl.pallas_call(
        _out_proj_kernel,
        grid=(t // tm,),
        in_specs=[pl.BlockSpec((tm, D_MODEL), row),
                  pl.BlockSpec((tm, k), row),
                  pl.BlockSpec((k, D_MODEL), lambda i: (0, 0))],
        out_specs=pl.BlockSpec((tm, D_MODEL), row),
        out_shape=jax.ShapeDtypeStruct((t, D_MODEL), F32),
        compiler_params=_params("parallel"),
        name="out_project_residual",
    )(h, a_bf16, w_bf16)


def _sb_prompt_kernel(q_ref, kt_ref, v_ref, bias_ref, tri_ref, o_ref, surv_ref, acc_ref):
    blk = SB_BLOCK
    tq = q_ref.shape[1]
    band = tq // blk
    qi = pl.program_id(2)
    bias = bias_ref[0]
    first_lanes = lax.broadcasted_iota(jnp.int32, (blk, LANES), 1) < SB_HEAD_DIM
    first_rows = lax.broadcasted_iota(jnp.int32, (LANES, blk), 0) < SB_HEAD_DIM

    surv_ref[...] = jnp.zeros_like(surv_ref)
    acc_ref[...] = jnp.zeros_like(acc_ref)

    def fold(j, row0, diagonal):
        rows = tq - row0
        start = pl.multiple_of(j * blk, blk)
        kt = kt_ref[j]
        v = v_ref[0, pl.ds(start, blk), :]
        zero = jnp.zeros_like(kt)
        kt2 = jnp.concatenate([jnp.where(first_rows, kt, zero), jnp.where(first_rows, zero, kt)], axis=1)
        v2 = jnp.concatenate([jnp.where(first_lanes, v, zero), jnp.where(first_lanes, zero, v)], axis=0)
        z2 = jnp.dot(q_ref[0, row0:tq, :], kt2, preferred_element_type=F32) * LOG2E + bias
        sp = _softplus_log2(z2)
        if diagonal:
            query = lax.broadcasted_iota(jnp.int32, (rows, 2 * blk), 0)
            key = lax.broadcasted_iota(jnp.int32, (rows, 2 * blk), 1) & (blk - 1)
            strictly_earlier = key < query
            sp = jnp.where(strictly_earlier, sp, 0.0)
        hi, lo = _split2(sp)
        ra = jnp.dot(jnp.concatenate([hi[:, :blk], lo[:, :blk]], axis=1), tri_ref[...],
                     preferred_element_type=F32)
        rb = jnp.dot(jnp.concatenate([hi[:, blk:], lo[:, blk:]], axis=1), tri_ref[...],
                     preferred_element_type=F32)
        suffix = jnp.concatenate([ra[:, :blk], rb[:, :blk]], axis=1)
        total = jnp.concatenate([ra[:, blk:], rb[:, blk:]], axis=1)
        w = jnp.exp2(z2 - suffix - surv_ref[row0:tq, :])
        if diagonal:
            w = jnp.where(strictly_earlier, w, 0.0)
        acc_ref[row0:tq, :] += jnp.dot(w.astype(BF16), v2, preferred_element_type=F32)
        surv_ref[row0:tq, :] += total

    def earlier_body(jj, carry):
        for u in range(band):
            fold((qi - 1 - jj) * band + band - 1 - u, 0, False)
        return carry

    for b in reversed(range(band)):
        fold(qi * band + b, b * blk, True)
    lax.fori_loop(0, qi, earlier_body, 0)
    o_ref[0] = acc_ref[...].astype(o_ref.dtype)


def _suffix_sum_matrix(blk):
    j = jnp.arange(blk)[:, None]
    s = jnp.arange(blk)[None, :]
    tri = (j >= s).astype(BF16)
    half = jnp.concatenate([tri, jnp.ones((blk, blk), BF16)], axis=1)
    return jnp.concatenate([half, half], axis=0)


def sb_attention_prompt(q, ktb, vb, bias):
    b, s, _ = q.shape
    blk = SB_BLOCK
    tq = SB_QUERY_TILE
    assert blk == PAGE_SIZE and s % tq == 0
    n_pairs = SB_HEADS // 2
    bias2 = jnp.repeat(bias.astype(F32).reshape(n_pairs, 2) * LOG2E, blk, axis=1).reshape(n_pairs, 1, 2 * blk)
    return pl.pallas_call(
        _sb_prompt_kernel,
        grid=(b, n_pairs, s // tq),
        in_specs=[pl.BlockSpec((1, tq, LANES), lambda bi, hp, qi: (bi, qi, hp)),
                  pl.BlockSpec((s // blk, LANES, blk), lambda bi, hp, qi: (bi, hp, 0)),
                  pl.BlockSpec((1, s, LANES), lambda bi, hp, qi: (bi, 0, hp)),
                  pl.BlockSpec((1, 1, 2 * blk), lambda bi, hp, qi: (hp, 0, 0)),
                  pl.BlockSpec((2 * blk, 2 * blk), lambda bi, hp, qi: (0, 0))],
        out_specs=pl.BlockSpec((1, tq, LANES), lambda bi, hp, qi: (bi, qi, hp)),
        out_shape=jax.ShapeDtypeStruct((b, s, D_MODEL), BF16),
        scratch_shapes=[pltpu.VMEM((tq, 2 * blk), F32), pltpu.VMEM((tq, LANES), F32)],
        compiler_params=_params("parallel", "parallel", "arbitrary"),
        name="sb_attention_prompt",
    )(q, ktb, vb, bias2, _suffix_sum_matrix(blk))


def _sb_sample_kernel(n_new, pt_ref, qbd_ref, bias_ref, kn_ref, vn_ref, tri_ref, *rest):
    n_pg = PAGES_PER_STEP
    k_refs = rest[:n_pg]
    v_refs = rest[n_pg:2 * n_pg]
    o_ref, surv_ref, acc_ref = rest[2 * n_pg:]
    step = pl.program_id(1)
    qbd = qbd_ref[0]
    bias = bias_ref[...]
    pg = PAGE_SIZE
    rows = qbd.shape[0]

    def scores(kt, visible):
        z2 = jnp.dot(qbd, kt, preferred_element_type=F32) * LOG2E + bias
        sp = _softplus_log2(z2)
        if visible is not None:
            sp = jnp.where(visible, sp, 0.0)
        hi, lo = _split2(sp)
        r = jnp.dot(jnp.concatenate([hi, lo], axis=1), tri_ref[...], preferred_element_type=F32)
        return z2, r[:, :pg], r[:, pg:]

    def weighted_values(z2, suffix, surv, vt, visible):
        w = jnp.exp2(z2 - suffix - surv)
        if visible is not None:
            w = jnp.where(visible, w, 0.0)
        return lax.dot_general(w.astype(BF16), vt, NT_DIMS, preferred_element_type=F32)

    @pl.when(step == 0)
    def _():
        query = lax.broadcasted_iota(jnp.int32, (rows, pg), 0) // SB_HEADS
        key = lax.broadcasted_iota(jnp.int32, (rows, pg), 1)
        visible = (key < query) & (key < n_new)
        z2, suffix, total = scores(kn_ref[0], visible)
        acc_ref[...] = weighted_values(z2, suffix, jnp.zeros_like(total), vn_ref[0], visible)
        surv_ref[...] = total

    parts = [scores(k_refs[i][0].astype(BF16), None) for i in range(n_pg)]
    surv = surv_ref[...]
    acc = acc_ref[...]
    for i, (z2, suffix, total) in enumerate(parts):
        acc = acc + weighted_values(z2, suffix, surv, v_refs[i][0].astype(BF16), None)
        surv = surv + total
    acc_ref[...] = acc
    surv_ref[...] = surv

    @pl.when(step == pl.num_programs(1) - 1)
    def _():
        head_of_row = lax.broadcasted_iota(jnp.int32, (SB_HEADS, D_MODEL), 0)
        head_of_lane = lax.broadcasted_iota(jnp.int32, (SB_HEADS, D_MODEL), 1) // SB_HEAD_DIM
        out = []
        for t in range(o_ref.shape[1]):
            if t < n_new:
                blk = acc_ref[t * SB_HEADS:(t + 1) * SB_HEADS, :]
                out.append(jnp.sum(jnp.where(head_of_row == head_of_lane, blk, 0.0), axis=0, keepdims=True))
            else:
                out.append(jnp.zeros((1, D_MODEL), F32))
        o_ref[0] = jnp.concatenate(out, axis=0).astype(o_ref.dtype)


def sb_attention_sample(q, kt_new, vt_new, cache_kt, cache_vt, page_table, bias):
    b, t, _ = q.shape
    n_pages = page_table.shape[1]
    n_pg = PAGES_PER_STEP
    rows = t * SB_HEADS
    assert n_pages % n_pg == 0 and rows % 16 == 0 and t <= 8
    t_pad = 8
    q4 = q.reshape(b, t, SB_HEADS, SB_HEAD_DIM)
    eye = jnp.eye(SB_HEADS, dtype=BF16)
    qbd = jnp.einsum("bthd,hg->bthgd", q4, eye).reshape(b, rows, D_MODEL)
    bias_rows = jnp.tile(bias.astype(F32) * LOG2E, t).reshape(rows, 1)
    new_pages = lambda a: jnp.pad(a.reshape(D_MODEL, b, t).transpose(1, 0, 2),
                                  ((0, 0), (0, 0), (0, PAGE_SIZE - t))).astype(BF16)

    def page_map(i):
        return lambda bi, si, pt: (pt[bi * n_pages + (n_pages - 1 - (si * n_pg + i))], 0, 0)

    per_b = lambda bi, si, pt: (bi, 0, 0)
    fixed = lambda bi, si, pt: (0, 0)
    page_specs = [pl.BlockSpec((1, D_MODEL, PAGE_SIZE), page_map(i)) for i in range(n_pg)]
    grid_spec = pltpu.PrefetchScalarGridSpec(
        num_scalar_prefetch=1,
        grid=(b, n_pages // n_pg),
        in_specs=[pl.BlockSpec((1, rows, D_MODEL), per_b),
                  pl.BlockSpec((rows, 1), fixed),
                  pl.BlockSpec((1, D_MODEL, PAGE_SIZE), per_b),
                  pl.BlockSpec((1, D_MODEL, PAGE_SIZE), per_b),
                  pl.BlockSpec((2 * PAGE_SIZE, 2 * PAGE_SIZE), fixed)] + page_specs + page_specs,
        out_specs=pl.BlockSpec((1, t_pad, D_MODEL), per_b),
        scratch_shapes=[pltpu.VMEM((rows, PAGE_SIZE), F32), pltpu.VMEM((rows, D_MODEL), F32)],
    )
    o = pl.pallas_call(
        functools.partial(_sb_sample_kernel, t),
        grid_spec=grid_spec,
        out_shape=jax.ShapeDtypeStruct((b, t_pad, D_MODEL), BF16),
        compiler_params=_params("parallel", "arbitrary"),
        name="sb_attention_sample",
    )(page_table.reshape(-1).astype(jnp.int32), qbd, bias_rows, new_pages(kt_new), new_pages(vt_new),
      _suffix_sum_matrix(PAGE_SIZE), *([cache_kt] * n_pg), *([cache_vt] * n_pg))
    return o[:, :t]


def _ssd_kernel(valid_len, has_init,
                x_ref, bm_ref, cm_ref, z_ref, dt_ref,
                cwx_ref, cwb_ref, cwc_ref, cbx_ref, cbb_ref, cbc_ref,
                px_ref, pb_ref, pc_ref, init_ref,
                dtb_ref, a_ref, aexp_ref, dexp_ref, nw_ref, expand_ref,
                y_ref, fin_ref,
                ext_x, ext_b, ext_c, state_ref):
    lc = x_ref.shape[1]
    c = pl.program_id(2)
    tail = 8

    @pl.when(c == 0)
    def _():
        ext_x[0:tail, :] = px_ref[0]
        ext_b[0:tail, :] = pb_ref[0]
        ext_c[0:tail, :] = pc_ref[0]
        if has_init:
            n = SSM_HEADS_PER_GROUP * SSM_HEAD_DIM
            state_ref[...] = init_ref[0].reshape(n, SSM_D_STATE).T
        else:
            state_ref[...] = jnp.zeros_like(state_ref)

    def conv_silu(cur_ref, ext, w_ref, b_ref):
        ext[tail:tail + lc, :] = cur_ref[0]
        acc = b_ref[...] + w_ref[SSM_CONV_W - 1:SSM_CONV_W, :] * ext[tail:tail + lc, :]
        for back in range(1, SSM_CONV_W):
            tap = SSM_CONV_W - 1 - back
            acc = acc + w_ref[tap:tap + 1, :] * ext[tail - back:tail - back + lc, :]
        ext[0:tail, :] = ext[lc:lc + tail, :]
        return acc * _sigmoid(acc)

    xs = conv_silu(x_ref, ext_x, cwx_ref, cbx_ref)
    bm = conv_silu(bm_ref, ext_b, cwb_ref, cbb_ref)
    cm = conv_silu(cm_ref, ext_c, cwc_ref, cbc_ref)

    dt = _softplus(dt_ref[0] + dtb_ref[0])
    if valid_len < lc:
        dt = jnp.where(lax.broadcasted_iota(jnp.int32, dt.shape, 0) < valid_len, dt, 0.0)
    t_idx = lax.broadcasted_iota(jnp.int32, (lc, lc), 0)
    s_idx = lax.broadcasted_iota(jnp.int32, (lc, lc), 1)
    causal = s_idx <= t_idx
    lower = causal.astype(BF16)

    def cumsum_rows(v):
        parts = _split3(v)
        return sum(jnp.dot(lower, p, preferred_element_type=F32) for p in parts)

    a_cum = cumsum_rows(dt * a_ref[0])
    a_cum_t = a_cum.T
    expand = expand_ref[...]
    dt_wide = sum(jnp.dot(p, expand, preferred_element_type=F32) for p in _split3(dt))
    a_cum_wide = cumsum_rows(dt_wide * aexp_ref[...])
    a_last_wide = a_cum_wide[lc - 1:lc, :]

    bm_t = bm.T.astype(BF16)
    cm_b = cm.astype(BF16)
    cb = jnp.dot(cm_b, bm_t, preferred_element_type=F32)
    xdt = xs * dt_wide
    xdt_b = xdt.astype(BF16)
    first_half = lax.broadcasted_iota(jnp.int32, (lc, LANES), 1) < SSM_HEAD_DIM
    y_parts = []
    for pair in range(SSM_HEADS_PER_GROUP // 2):
        halves = []
        for k in (2 * pair, 2 * pair + 1):
            seg = a_cum[:, k:k + 1] - a_cum_t[k:k + 1, :]
            m = (cb * jnp.exp(jnp.where(causal, seg, -jnp.inf))).astype(BF16)
            halves.append(jnp.dot(m, xdt_b[:, pair * LANES:(pair + 1) * LANES], preferred_element_type=F32))
        y_parts.append(jnp.where(first_half, halves[0], halves[1]))
    y = jnp.concatenate(y_parts, axis=1)

    state = state_ref[...]
    y = y + jnp.dot(cm_b, state.astype(BF16), preferred_element_type=F32) * jnp.exp(a_cum_wide)
    y = y + dexp_ref[...] * xs
    y = y * (z_ref[0] * _sigmoid(z_ref[0]))
    y = y * lax.rsqrt(jnp.mean(y * y, axis=-1, keepdims=True) + RMS_EPS) * nw_ref[...]
    y_ref[0] = y.astype(y_ref.dtype)

    x_to_end = (xdt * jnp.exp(a_last_wide - a_cum_wide)).astype(BF16)
    new_state = state * jnp.exp(a_last_wide) + jnp.dot(bm_t, x_to_end, preferred_element_type=F32)
    state_ref[...] = new_state

    @pl.when(c == pl.num_programs(2) - 1)
    def _():
        fin_ref[0] = new_state.T.reshape(SSM_HEADS_PER_GROUP, SSM_HEAD_DIM, SSM_D_STATE)


def ssd_mixer(z, xbc, dt, conv_prev, init_state, valid_len, p):
    b, l, _ = z.shape
    lc = SSM_CHUNK
    assert l % lc == 0
    n_x = SSM_D_INNER // SSM_GROUP_DIM
    gd, ns = SSM_GROUP_DIM, SSM_D_STATE
    b_off = SSM_D_INNER // ns
    c_off = b_off + SSM_GROUPS
    has_init = init_state is not None
    if not has_init:
        init_state = jnp.zeros((1, SSM_HEADS, SSM_HEAD_DIM, ns), F32)

    seq = lambda off: (lambda bi, g, c: (bi, c, off + g))
    par = lambda off: (lambda bi, g, c: (0, off + g))
    prev = lambda off: (lambda bi, g, c: (bi, 0, off + g))
    init_map = (lambda bi, g, c: (bi, g, 0, 0)) if has_init else (lambda bi, g, c: (0, g, 0, 0))
    hg = SSM_HEADS_PER_GROUP
    in_specs = [
        pl.BlockSpec((1, lc, gd), seq(0)), pl.BlockSpec((1, lc, ns), seq(b_off)), pl.BlockSpec((1, lc, ns), seq(c_off)),
        pl.BlockSpec((1, lc, gd), seq(0)), pl.BlockSpec((1, lc, LANES), seq(0)),
        pl.BlockSpec((SSM_CONV_W, gd), par(0)), pl.BlockSpec((SSM_CONV_W, ns), par(b_off)),
        pl.BlockSpec((SSM_CONV_W, ns), par(c_off)),
        pl.BlockSpec((1, gd), par(0)), pl.BlockSpec((1, ns), par(b_off)), pl.BlockSpec((1, ns), par(c_off)),
        pl.BlockSpec((1, 8, gd), prev(0)), pl.BlockSpec((1, 8, ns), prev(b_off)), pl.BlockSpec((1, 8, ns), prev(c_off)),
        pl.BlockSpec((1, hg, SSM_HEAD_DIM, ns), init_map),
        pl.BlockSpec((1, 1, LANES), lambda bi, g, c: (g, 0, 0)),
        pl.BlockSpec((1, 1, LANES), lambda bi, g, c: (g, 0, 0)),
        pl.BlockSpec((1, gd), par(0)), pl.BlockSpec((1, gd), par(0)), pl.BlockSpec((1, gd), par(0)),
        pl.BlockSpec((LANES, gd), lambda bi, g, c: (0, 0)),
    ]
    out_specs = [pl.BlockSpec((1, lc, gd), seq(0)),
                 pl.BlockSpec((1, hg, SSM_HEAD_DIM, ns), lambda bi, g, c: (bi, g, 0, 0))]
    return pl.pallas_call(
        functools.partial(_ssd_kernel, valid_len, has_init),
        grid=(b, SSM_GROUPS, l // lc),
        in_specs=in_specs,
        out_specs=out_specs,
        out_shape=[jax.ShapeDtypeStruct((b, l, SSM_D_INNER), BF16),
                   jax.ShapeDtypeStruct((b, SSM_HEADS, SSM_HEAD_DIM, ns), F32)],
        scratch_shapes=[pltpu.VMEM((lc + 8, gd), F32), pltpu.VMEM((lc + 8, ns), F32),
                        pltpu.VMEM((lc + 8, ns), F32), pltpu.VMEM((ns, gd), F32)],
        compiler_params=_params("parallel", "parallel", "arbitrary"),
        name="ssd_mixer",
    )(xbc, xbc, xbc, z, dt,
      p["conv_w"], p["conv_w"], p["conv_w"], p["conv_b"], p["conv_b"], p["conv_b"],
      conv_prev, conv_prev, conv_prev, init_state,
      p["dt_bias"], p["a_neg"], p["a_wide"], p["d_wide"], p["norm_w"], p["expand"])


def ssm_params(w_in, conv_w, conv_b, dt_bias, a_log, d_skip, norm_w, w_out):
    hg = SSM_HEADS_PER_GROUP
    w_dt = w_in[:, SSM_D_INNER + SSM_CONV_DIM:].reshape(D_MODEL, SSM_GROUPS, hg)
    w_dt = jnp.pad(w_dt, ((0, 0), (0, 0), (0, LANES - hg))).reshape(D_MODEL, SSM_DT_COLS)
    lane_pad = lambda v: jnp.pad(v.astype(F32).reshape(SSM_GROUPS, 1, hg), ((0, 0), (0, 0), (0, LANES - hg)))
    a_neg = -jnp.exp(a_log.astype(F32))
    head_of_chan = jnp.arange(SSM_GROUP_DIM) // SSM_HEAD_DIM
    expand = (jnp.arange(LANES)[:, None] == head_of_chan[None, :]).astype(BF16)
    return {
        "w_z": w_in[:, :SSM_D_INNER].astype(BF16),
        "w_xbc": w_in[:, SSM_D_INNER:SSM_D_INNER + SSM_CONV_DIM].astype(BF16),
        "w_dt": w_dt.astype(BF16),
        "conv_w": conv_w.astype(F32), "conv_b": conv_b.astype(F32).reshape(1, SSM_CONV_DIM),
        "dt_bias": lane_pad(dt_bias), "a_neg": lane_pad(a_neg),
        "a_wide": jnp.repeat(a_neg, SSM_HEAD_DIM).reshape(1, SSM_D_INNER),
        "d_wide": jnp.repeat(d_skip.astype(F32), SSM_HEAD_DIM).reshape(1, SSM_D_INNER),
        "norm_w": norm_w.astype(F32).reshape(1, SSM_D_INNER),
        "expand": expand,
        "w_out": w_out.astype(BF16),
    }


def _router_kernel(h_ref, g_ref, w_ref, b_ref, before_ref, xn_ref, meta_ref, cnt_ref, carry_ref):
    step = pl.program_id(0)
    tm = h_ref.shape[0]

    @pl.when(step == 0)
    def _():
        carry_ref[...] = jnp.zeros_like(carry_ref)

    xn = _rms_norm(h_ref[...], g_ref[...])
    xn_ref[...] = xn
    logits = lax.dot_general(w_ref[...], xn.astype(BF16), NT_DIMS, preferred_element_type=F32) + b_ref[...]

    sub8 = lax.broadcasted_iota(jnp.int32, (8, tm), 0)
    neg_inf = jnp.float32(-jnp.inf)

    def first_argmax(v, vmax):
        return jnp.min(jnp.where(v == vmax, sub8, 8), axis=0, keepdims=True)

    grp = jnp.where(sub8 < MOE_GROUPS, logits[0:8], neg_inf)
    g_max = jnp.max(grp, axis=0, keepdims=True)
    g_idx = first_argmax(grp, g_max)
    g_prob = 1.0 / jnp.sum(jnp.exp(grp - g_max), axis=0, keepdims=True)

    in_group = logits[8:16]
    for g in range(1, MOE_GROUPS):
        in_group = jnp.where(g_idx == g, logits[8 + 8 * g:16 + 8 * g], in_group)
    m1 = jnp.max(in_group, axis=0, keepdims=True)
    i1 = first_argmax(in_group, m1)
    rest = jnp.where(sub8 == i1, neg_inf, in_group)
    m2 = jnp.max(rest, axis=0, keepdims=True)
    i2 = first_argmax(rest, m2)
    p2 = jnp.exp(m2 - m1)
    w1 = g_prob / (1.0 + p2)
    w2 = g_prob * p2 / (1.0 + p2)
    e1 = g_idx * MOE_EXPERTS_PER_GROUP + i1
    e2 = g_idx * MOE_EXPERTS_PER_GROUP + i2

    sub_e = lax.broadcasted_iota(jnp.int32, (MOE_EXPERTS, tm), 0)
    hot1 = sub_e == e1
    hot2 = sub_e == e2
    hot = jnp.concatenate([hot1, hot2], axis=0).astype(BF16)
    earlier = jnp.dot(hot, before_ref[...], preferred_element_type=F32)
    hot1f = hot1.astype(F32)
    hot2f = hot2.astype(F32)
    cnt1 = jnp.sum(hot1f, axis=1, keepdims=True)
    cnt2 = jnp.sum(hot2f, axis=1, keepdims=True)
    carry = carry_ref[...]
    r1 = jnp.sum(hot1f * (earlier[:MOE_EXPERTS] + carry), axis=0, keepdims=True)
    r2 = jnp.sum(hot2f * (earlier[MOE_EXPERTS:] + carry + cnt1), axis=0, keepdims=True)
    carry = carry + cnt1 + cnt2
    carry_ref[...] = carry
    cnt_ref[...] = carry
    zero = jnp.zeros((1, tm), F32)
    meta_ref[...] = jnp.concatenate(
        [e1.astype(F32), e2.astype(F32), w1, w2, r1, r2, zero, zero], axis=0)


def moe_route(h, gain, r):
    t = h.shape[0]
    tm = _row_tile(t, 512)
    i = jnp.arange(tm)
    before = (i[:, None] < i[None, :]).astype(BF16)
    fixed = lambda s: (0, 0)
    return pl.pallas_call(
        _router_kernel,
        grid=(t // tm,),
        in_specs=[pl.BlockSpec((tm, D_MODEL), lambda s: (s, 0)),
                  pl.BlockSpec((1, D_MODEL), fixed),
                  pl.BlockSpec((ROUTE_ROWS, D_MODEL), fixed),
                  pl.BlockSpec((ROUTE_ROWS, 1), fixed),
                  pl.BlockSpec((tm, tm), fixed)],
        out_specs=[pl.BlockSpec((tm, D_MODEL), lambda s: (s, 0)),
                   pl.BlockSpec((8, tm), lambda s: (0, s)),
                   pl.BlockSpec((MOE_EXPERTS, 1), fixed)],
        out_shape=[jax.ShapeDtypeStruct((t, D_MODEL), F32),
                   jax.ShapeDtypeStruct((8, t), F32),
                   jax.ShapeDtypeStruct((MOE_EXPERTS, 1), F32)],
        scratch_shapes=[pltpu.VMEM((MOE_EXPERTS, 1), F32)],
        compiler_params=_params("arbitrary"),
        name="moe_route",
    )(h, gain, r["w_route"], r["bias"], before)


def _scatter_rows_kernel(pos_ref, x_ref, rows_in_ref, rows_ref, sem):
    del rows_in_ref
    tm = x_ref.shape[0]

    def copy(k, t):
        return pltpu.make_async_copy(x_ref.at[pl.ds(t, 1), :], rows_ref.at[pl.ds(pos_ref[k, t], 1), :], sem)

    def start(t, carry):
        copy(0, t).start()
        copy(1, t).start()
        return carry

    def wait(t, carry):
        copy(0, t).wait()
        copy(1, t).wait()
        return carry

    lax.fori_loop(0, tm, start, 0, unroll=ROW_DMA_UNROLL)
    lax.fori_loop(0, tm, wait, 0, unroll=ROW_DMA_UNROLL)


def moe_scatter_rows(xn, pos, n_rows):
    t = xn.shape[0]
    tm = _row_tile(t, 512)
    rows0 = jnp.zeros((n_rows, D_MODEL), F32)
    return pl.pallas_call(
        _scatter_rows_kernel,
        grid=(t // tm,),
        in_specs=[pl.BlockSpec((2, tm), lambda s: (0, s), memory_space=pltpu.SMEM),
                  pl.BlockSpec((tm, D_MODEL), lambda s: (s, 0)),
                  pl.BlockSpec(memory_space=pl.ANY)],
        out_specs=pl.BlockSpec(memory_space=pl.ANY),
        out_shape=jax.ShapeDtypeStruct((n_rows, D_MODEL), F32),
        scratch_shapes=[pltpu.SemaphoreType.DMA(())],
        input_output_aliases={2: 0},
        compiler_params=_params("arbitrary"),
        name="moe_scatter_rows",
    )(pos, xn, rows0)


def _expert_mlp_kernel(be_ref, nb_ref, x_ref, wgu_ref, wd_ref, o_ref, wgu_bf, wd_bf):
    i = pl.program_id(0)

    @pl.when(i < nb_ref[0])
    def _():
        @pl.when((i == 0) | (be_ref[i] != be_ref[jnp.maximum(i - 1, 0)]))
        def _():
            wgu_bf[...] = wgu_ref[0, 0].astype(BF16)
            wd_bf[...] = wd_ref[0, 0].astype(BF16)

        x = x_ref[...].astype(BF16)
        gate = jnp.dot(x, wgu_bf[:, :MOE_D_FF], preferred_element_type=F32)
        up = jnp.dot(x, wgu_bf[:, MOE_D_FF:], preferred_element_type=F32)
        act = (gate * _sigmoid(gate) * up).astype(BF16)
        o_ref[...] = jnp.dot(act, wd_bf[...], preferred_element_type=F32)

    @pl.when(i >= nb_ref[0])
    def _():
        o_ref[...] = jnp.zeros_like(o_ref)


def moe_expert_mlp(rows, block_expert, n_used, w_gu, w_down, layer):
    n_rows = rows.shape[0]
    br = MOE_BLOCK_ROWS
    grid_spec = pltpu.PrefetchScalarGridSpec(
        num_scalar_prefetch=2,
        grid=(n_rows // br,),
        in_specs=[pl.BlockSpec((br, D_MODEL), lambda i, be, nb: (i, 0)),
                  pl.BlockSpec((1, 1, D_MODEL, 2 * MOE_D_FF), lambda i, be, nb: (layer, be[i], 0, 0)),
                  pl.BlockSpec((1, 1, MOE_D_FF, D_MODEL), lambda i, be, nb: (layer, be[i], 0, 0))],
        out_specs=pl.BlockSpec((br, D_MODEL), lambda i, be, nb: (i, 0)),
        scratch_shapes=[pltpu.VMEM((D_MODEL, 2 * MOE_D_FF), BF16), pltpu.VMEM((MOE_D_FF, D_MODEL), BF16)],
    )
    return pl.pallas_call(
        _expert_mlp_kernel,
        grid_spec=grid_spec,
        out_shape=jax.ShapeDtypeStruct((n_rows, D_MODEL), F32),
        compiler_params=_params("arbitrary"),
        name="moe_expert_mlp",
    )(block_expert, n_used, rows, w_gu, w_down)


def _combine_kernel(final_norm, pos_ref, h_ref, w_ref, rows_ref, g_ref, o_ref, y_ref, buf1, buf2, sem):
    tm = h_ref.shape[0]

    def copies(t):
        return (pltpu.make_async_copy(rows_ref.at[pl.ds(pos_ref[0, t], 1), :], buf1.at[pl.ds(t, 1), :], sem),
                pltpu.make_async_copy(rows_ref.at[pl.ds(pos_ref[1, t], 1), :], buf2.at[pl.ds(t, 1), :], sem))

    def start(t, carry):
        c1, c2 = copies(t)
        c1.start()
        c2.start()
        return carry

    def wait(t, carry):
        c1, c2 = copies(t)
        c1.wait()
        c2.wait()
        return carry

    lax.fori_loop(0, tm, start, 0, unroll=ROW_DMA_UNROLL)
    lax.fori_loop(0, tm, wait, 0, unroll=ROW_DMA_UNROLL)
    w = w_ref[...]
    out = h_ref[...] + (w[:, 0:1] * buf1[...] + w[:, 1:2] * buf2[...])
    o_ref[...] = out
    if final_norm:
        y_ref[...] = _rms_norm(out, g_ref[...])
    else:
        y_ref[...] = jnp.zeros_like(y_ref)


def moe_combine(h, pos, w_cols, out_rows, final_gain):
    t = h.shape[0]
    tm = _row_tile(t, 256)
    final_norm = final_gain is not None
    gain = final_gain if final_norm else jnp.ones((1, D_MODEL), F32)
    y_rows = tm if final_norm else 8
    y_shape = (t, D_MODEL) if final_norm else (8 * (t // tm), D_MODEL)
    out, y = pl.pallas_call(
        functools.partial(_combine_kernel, final_norm),
        grid=(t // tm,),
        in_specs=[pl.BlockSpec((2, tm), lambda s: (0, s), memory_space=pltpu.SMEM),
                  pl.BlockSpec((tm, D_MODEL), lambda s: (s, 0)),
                  pl.BlockSpec((tm, 2), lambda s: (s, 0)),
                  pl.BlockSpec(memory_space=pl.ANY),
                  pl.BlockSpec((1, D_MODEL), lambda s: (0, 0))],
        out_specs=[pl.BlockSpec((tm, D_MODEL), lambda s: (s, 0)),
                   pl.BlockSpec((y_rows, D_MODEL), lambda s: (s, 0))],
        out_shape=[jax.ShapeDtypeStruct((t, D_MODEL), F32), jax.ShapeDtypeStruct(y_shape, F32)],
        scratch_shapes=[pltpu.VMEM((tm, D_MODEL), F32), pltpu.VMEM((tm, D_MODEL), F32),
                        pltpu.SemaphoreType.DMA(())],
        compiler_params=_params("arbitrary"),
        name="moe_combine",
    )(pos, h, w_cols, out_rows, gain)
    return out, (y if final_norm else None)


def moe_params(w_group, b_group, w_route, b_route):
    w_rt = jnp.zeros((ROUTE_ROWS, D_MODEL), F32)
    w_rt = w_rt.at[0:MOE_GROUPS].set(w_group.astype(F32).T)
    w_rt = w_rt.at[8:8 + MOE_EXPERTS].set(w_route.astype(F32).T)
    bias = jnp.zeros((ROUTE_ROWS,), F32)
    bias = bias.at[0:MOE_GROUPS].set(b_group.astype(F32)).at[8:8 + MOE_EXPERTS].set(b_route.astype(F32))
    return {"w_route": w_rt.astype(BF16), "bias": bias.reshape(ROUTE_ROWS, 1)}


def hier_moe_residual(h, gain, r, w_gu, w_down, layer, final_gain=None):
    t = h.shape[0]
    br = MOE_BLOCK_ROWS
    xn, meta, counts = moe_route(h, gain, r)
    counts = counts[:, 0].astype(jnp.int32)
    padded = (counts + br - 1) // br * br
    padded_end = jnp.cumsum(padded)
    padded_start = padded_end - padded
    expert = meta[0:2].astype(jnp.int32)
    hot = expert[..., None] == jnp.arange(MOE_EXPERTS, dtype=jnp.int32)
    pos = jnp.sum(jnp.where(hot, padded_start, 0), axis=-1) + meta[4:6].astype(jnp.int32)
    n_blocks = -(-(t * MOE_TOP_K) // br) + MOE_EXPERTS
    block_start = jnp.arange(n_blocks, dtype=jnp.int32) * br
    block_expert = jnp.minimum(jnp.sum(padded_end[None, :] <= block_start[:, None], axis=1), MOE_EXPERTS - 1)
    n_used = (padded_end[-1] // br).reshape(1)
    rows = moe_scatter_rows(xn, pos, n_blocks * br)
    out_rows = moe_expert_mlp(rows, block_expert.astype(jnp.int32), n_used.astype(jnp.int32), w_gu, w_down, layer)
    return moe_combine(h, pos, meta[2:4].T, out_rows, final_gain)


def kernel(x_prompt, x_sample, cache_k, cache_v, page_table, state_ssm, state_conv, norm_mix, norm_ffn, norm_final,
           sb_w_qkv, sb_bias, sb_w_o, ssm_w_in, ssm_conv_w, ssm_conv_b, ssm_dt_bias, ssm_a_log, ssm_d, ssm_norm_w,
           ssm_w_out, moe_w_group, moe_b_group, moe_w_route, moe_b_route, moe_w_gu, moe_w_down):
    bp, sp, _ = x_prompt.shape
    bs, ts, _ = x_sample.shape
    depth = norm_mix.shape[0]
    hp = x_prompt.reshape(bp * sp, D_MODEL)
    hs = x_sample.reshape(bs * ts, D_MODEL)
    n_phys = cache_k.shape[1]
    gain = lambda g: g.astype(F32).reshape(1, D_MODEL)
    outs = {name: [] for name in ("k_p", "v_p", "k_s", "v_s", "ssm_p", "conv_p", "ssm_s", "conv_s")}
    yp = ys = None
    for layer in range(depth):
        slot = layer // N_MIXERS
        g_mix = gain(norm_mix[layer])
        if layer % N_MIXERS == 0:
            w = sb_w_qkv[slot]
            w_qv = jnp.concatenate([w[:, :D_MODEL], w[:, 2 * D_MODEL:]], axis=1).astype(BF16)
            w_kv_t = w[:, D_MODEL:].T.astype(BF16)
            w_o = sb_w_o[slot].astype(BF16)
            q, kt, vt, ktb, vb = qkv_project(hp, g_mix, w_qv, w_kv_t)
            shape3 = (bp, sp, D_MODEL)
            o = sb_attention_prompt(q.reshape(shape3), ktb, vb.reshape(shape3), sb_bias[slot])
            hp = out_project_residual(hp, o.reshape(bp * sp, D_MODEL), w_o)
            pages = lambda a: a.reshape(bp, sp // PAGE_SIZE, SB_HEADS, SB_HEAD_DIM, PAGE_SIZE).transpose(0, 1, 4, 2, 3)
            outs["k_p"].append(pages(kt))
            outs["v_p"].append(pages(vt))

            q, kt, vt, _, _ = qkv_project(hs, g_mix, w_qv, w_kv_t)
            kt, vt = kt[0], vt[0]
            cache_t = lambda c: c[slot].transpose(0, 2, 3, 1).reshape(n_phys, D_MODEL, PAGE_SIZE)
            o = sb_attention_sample(q.reshape(bs, ts, D_MODEL), kt, vt, cache_t(cache_k), cache_t(cache_v),
                                    page_table, sb_bias[slot])
            hs = out_project_residual(hs, o.reshape(bs * ts, D_MODEL), w_o)
            rows = lambda a: a.reshape(SB_HEADS, SB_HEAD_DIM, bs, ts).transpose(2, 3, 0, 1)
            outs["k_s"].append(rows(kt))
            outs["v_s"].append(rows(vt))
        else:
            p = ssm_params(ssm_w_in[slot], ssm_conv_w[slot], ssm_conv_b[slot], ssm_dt_bias[slot], ssm_a_log[slot],
                           ssm_d[slot], ssm_norm_w[slot], ssm_w_out[slot])
            z, xbc, dt = ssm_in_project(hp, g_mix, p["w_z"], p["w_xbc"], p["w_dt"])
            xbc3 = xbc.reshape(bp, sp, SSM_CONV_DIM)
            y, fin = ssd_mixer(z.reshape(bp, sp, SSM_D_INNER), xbc3, dt.reshape(bp, sp, SSM_DT_COLS),
                               jnp.zeros((bp, 8, SSM_CONV_DIM), F32), None, SSM_CHUNK, p)
            hp = out_project_residual(hp, y.reshape(bp * sp, SSM_D_INNER), p["w_out"])
            outs["ssm_p"].append(fin)
            outs["conv_p"].append(xbc3[:, sp - (SSM_CONV_W - 1):])

            z, xbc, dt = ssm_in_project(hs, g_mix, p["w_z"], p["w_xbc"], p["w_dt"])
            pad_seq = lambda a: jnp.pad(a.reshape(bs, ts, -1), ((0, 0), (0, SSM_CHUNK - ts), (0, 0)))
            conv_prev = jnp.pad(state_conv[slot].astype(F32), ((0, 0), (8 - (SSM_CONV_W - 1), 0), (0, 0)))
            y, fin = ssd_mixer(pad_seq(z), pad_seq(xbc), pad_seq(dt), conv_prev, state_ssm[slot].astype(F32), ts, p)
            hs = out_project_residual(hs, y[:, :ts].reshape(bs * ts, SSM_D_INNER), p["w_out"])
            xbc_ext = jnp.concatenate([state_conv[slot].astype(F32), xbc.reshape(bs, ts, SSM_CONV_DIM)], axis=1)
            outs["ssm_s"].append(fin)
            outs["conv_s"].append(xbc_ext[:, ts:])
        r = moe_params(moe_w_group[layer], moe_b_group[layer], moe_w_route[layer], moe_b_route[layer])
        final_gain = gain(norm_final) if layer == depth - 1 else None
        hp, yp = hier_moe_residual(hp, gain(norm_ffn[layer]), r, moe_w_gu, moe_w_down, layer, final_gain)
        hs, ys = hier_moe_residual(hs, gain(norm_ffn[layer]), r, moe_w_gu, moe_w_down, layer, final_gain)
    return (yp.reshape(bp, sp, D_MODEL), ys.reshape(bs, ts, D_MODEL),
            jnp.stack(outs["k_p"]), jnp.stack(outs["v_p"]), jnp.stack(outs["k_s"]), jnp.stack(outs["v_s"]),
            jnp.stack(outs["ssm_p"]), jnp.stack(outs["conv_p"]), jnp.stack(outs["ssm_s"]), jnp.stack(outs["conv_s"]))
```

```python
import functools
import math

import jax
import jax.numpy as jnp
from jax import lax
from jax.experimental import pallas as pl
from jax.experimental.pallas import tpu as pltpu

F32 = jnp.float32
BF16 = jnp.bfloat16

D_MODEL = 1024
RMS_EPS = 1e-6
N_MIXERS = 2

SB_HEADS = 16
SB_HEAD_DIM = D_MODEL // SB_HEADS
SB_SCALE = 1.0 / math.sqrt(SB_HEAD_DIM)
SB_BLOCK = 128
SB_QUERY_TILE = 512
BIAS_PARTS = 3
PAGE_SIZE = 128
PAGES_PER_STEP = 16

SSM_D_INNER = 2 * D_MODEL
SSM_HEAD_DIM = 64
SSM_HEADS = SSM_D_INNER // SSM_HEAD_DIM
SSM_GROUPS = 4
SSM_HEADS_PER_GROUP = SSM_HEADS // SSM_GROUPS
SSM_GROUP_DIM = SSM_D_INNER // SSM_GROUPS
SSM_D_STATE = 128
SSM_CONV_W = 4
SSM_CONV_DIM = SSM_D_INNER + 2 * SSM_GROUPS * SSM_D_STATE
SSM_CHUNK = 128

MOE_GROUPS = 4
MOE_EXPERTS_PER_GROUP = 8
MOE_EXPERTS = MOE_GROUPS * MOE_EXPERTS_PER_GROUP
MOE_TOP_K = 2
MOE_D_FF = D_MODEL // 2
MOE_BLOCK_ROWS = 256
ROUTE_ROWS = 48
ROW_DMA_UNROLL = 8

LANES = 128
VMEM_LIMIT_BYTES = 48 * 1024 * 1024

NT_DIMS = (((1,), (1,)), ((), ()))


def _params(*semantics):
    return pltpu.CompilerParams(dimension_semantics=semantics, vmem_limit_bytes=VMEM_LIMIT_BYTES)


def _rms_norm(x, gain):
    return x * lax.rsqrt(jnp.mean(x * x, axis=-1, keepdims=True) + RMS_EPS) * gain


def _split2(x):
    hi = x.astype(BF16)
    lo = (x - hi.astype(F32)).astype(BF16)
    return hi, lo


def _split3(x):
    hi = x.astype(BF16)
    r = x - hi.astype(F32)
    mid = r.astype(BF16)
    lo = (r - mid.astype(F32)).astype(BF16)
    return hi, mid, lo


def _exact_bf16_parts(x):
    def truncate(v):
        return lax.bitcast_convert_type(lax.bitcast_convert_type(v, jnp.uint32) & jnp.uint32(0xFFFF0000), F32)
    hi = truncate(x)
    mid = truncate(x - hi)
    return hi, mid, x - hi - mid


def _softplus(z):
    return jnp.maximum(z, 0.0) + jnp.log(1.0 + jnp.exp(-jnp.abs(z)))


LOG2E = 1.4426950408889634


def _neg_abs(x):
    bits = lax.bitcast_convert_type(x, jnp.uint32) | jnp.uint32(0x80000000)
    return lax.bitcast_convert_type(bits, F32)


def _softplus_log2(z2):
    return jnp.maximum(z2, 0.0) + jnp.log2(1.0 + jnp.exp2(_neg_abs(z2)))


def _sigmoid(x):
    return 1.0 / (1.0 + jnp.exp(-x))


def _row_tile(t, want):
    return want if t % want == 0 else t


def _qkv_kernel(x_ref, g_ref, wq_ref, wkvt_ref, q_ref, kt_ref, vt_ref, ktb_ref, vb_ref):
    xn = _rms_norm(x_ref[...], g_ref[...]).astype(BF16)
    q = jnp.dot(xn, wq_ref[...], preferred_element_type=F32)
    kt = lax.dot_general(wkvt_ref[0:D_MODEL, :], xn, NT_DIMS, preferred_element_type=F32)
    vt = lax.dot_general(wkvt_ref[D_MODEL:2 * D_MODEL, :], xn, NT_DIMS, preferred_element_type=F32)
    q_ref[...] = (q * SB_SCALE).astype(BF16)
    vb_ref[...] = vt.T.astype(BF16)
    for p in range(kt_ref.shape[0]):
        cols = slice(p * PAGE_SIZE, (p + 1) * PAGE_SIZE)
        kt_ref[p] = kt[:, cols]
        vt_ref[p] = vt[:, cols]
        ktb_ref[p] = kt[:, cols].astype(BF16)


def qkv_project(h, gain, w_q, w_kv_t):
    t = h.shape[0]
    tm = _row_tile(t, 256)
    pages = tm // PAGE_SIZE
    row = lambda i: (i, 0)
    fixed = lambda i: (0, 0)
    page = lambda i: (i, 0, 0)
    return pl.pallas_call(
        _qkv_kernel,
        grid=(t // tm,),
        in_specs=[pl.BlockSpec((tm, D_MODEL), row),
                  pl.BlockSpec((1, D_MODEL), fixed),
                  pl.BlockSpec((D_MODEL, D_MODEL), fixed),
                  pl.BlockSpec((2 * D_MODEL, D_MODEL), fixed)],
        out_specs=[pl.BlockSpec((tm, D_MODEL), row),
                   pl.BlockSpec((pages, D_MODEL, PAGE_SIZE), page),
                   pl.BlockSpec((pages, D_MODEL, PAGE_SIZE), page),
                   pl.BlockSpec((pages, D_MODEL, PAGE_SIZE), page),
                   pl.BlockSpec((tm, D_MODEL), row)],
        out_shape=[jax.ShapeDtypeStruct((t, D_MODEL), BF16),
                   jax.ShapeDtypeStruct((t // PAGE_SIZE, D_MODEL, PAGE_SIZE), F32),
                   jax.ShapeDtypeStruct((t // PAGE_SIZE, D_MODEL, PAGE_SIZE), F32),
                   jax.ShapeDtypeStruct((t // PAGE_SIZE, D_MODEL, PAGE_SIZE), BF16),
                   jax.ShapeDtypeStruct((t, D_MODEL), BF16)],
        compiler_params=_params("parallel"),
        name="qkv_project",
    )(h, gain, w_q, w_kv_t)


SSM_DT_COLS = SSM_GROUPS * LANES


def _ssm_in_kernel(x_ref, g_ref, wz_ref, wx_ref, wd_ref, z_ref, xbc_ref, dt_ref):
    xn = _rms_norm(x_ref[...], g_ref[...]).astype(BF16)
    z_ref[...] = jnp.dot(xn, wz_ref[...], preferred_element_type=F32)
    xbc_ref[...] = jnp.dot(xn, wx_ref[...], preferred_element_type=F32)
    dt_ref[...] = jnp.dot(xn, wd_ref[...], preferred_element_type=F32)


def ssm_in_project(h, gain, w_z, w_xbc, w_dt):
    t = h.shape[0]
    tm = _row_tile(t, 256)
    row = lambda i: (i, 0)
    fixed = lambda i: (0, 0)
    return pl.pallas_call(
        _ssm_in_kernel,
        grid=(t // tm,),
        in_specs=[pl.BlockSpec((tm, D_MODEL), row),
                  pl.BlockSpec((1, D_MODEL), fixed),
                  pl.BlockSpec((D_MODEL, SSM_D_INNER), fixed),
                  pl.BlockSpec((D_MODEL, SSM_CONV_DIM), fixed),
                  pl.BlockSpec((D_MODEL, SSM_DT_COLS), fixed)],
        out_specs=[pl.BlockSpec((tm, SSM_D_INNER), row),
                   pl.BlockSpec((tm, SSM_CONV_DIM), row),
                   pl.BlockSpec((tm, SSM_DT_COLS), row)],
        out_shape=[jax.ShapeDtypeStruct((t, SSM_D_INNER), F32),
                   jax.ShapeDtypeStruct((t, SSM_CONV_DIM), F32),
                   jax.ShapeDtypeStruct((t, SSM_DT_COLS), F32)],
        compiler_params=_params("parallel"),
        name="ssm_in_project",
    )(h, gain, w_z, w_xbc, w_dt)


def _out_proj_kernel(h_ref, a_ref, w_ref, o_ref):
    o_ref[...] = h_ref[...] + jnp.dot(a_ref[...], w_ref[...], preferred_element_type=F32)


def out_project_residual(h, a_bf16, w_bf16):
    t, k = a_bf16.shape
    tm = _row_tile(t, 512)
    row = lambda i: (i, 0)
    return pl.pallas_call(
        _out_proj_kernel,
        grid=(t // tm,),
        in_specs=[pl.BlockSpec((tm, D_MODEL), row),
                  pl.BlockSpec((tm, k), row),
                  pl.BlockSpec((k, D_MODEL), lambda i: (0, 0))],
        out_specs=pl.BlockSpec((tm, D_MODEL), row),
        out_shape=jax.ShapeDtypeStruct((t, D_MODEL), F32),
        compiler_params=_params("parallel"),
        name="out_project_residual",
    )(h, a_bf16, w_bf16)


def _sb_prompt_kernel(q_ref, kt_ref, v_ref, bias_ref, tri_ref, o_ref, qa_ref, surv_ref, acc_ref):
    blk = SB_BLOCK
    tq = q_ref.shape[1]
    band = tq // blk
    qi = pl.program_id(2)
    first_lanes = lax.broadcasted_iota(jnp.int32, (blk, LANES), 1) < SB_HEAD_DIM
    first_rows = lax.broadcasted_iota(jnp.int32, (LANES, blk), 0) < SB_HEAD_DIM

    ones_cols = jnp.where(lax.broadcasted_iota(jnp.int32, (tq, LANES), 1) < BIAS_PARTS, 1.0, 0.0).astype(BF16)
    qa_ref[...] = jnp.concatenate([q_ref[0], ones_cols], axis=1)
    surv_ref[...] = jnp.zeros_like(surv_ref)
    acc_ref[...] = jnp.zeros_like(acc_ref)

    def fold(j, row0, diagonal):
        rows = tq - row0
        start = pl.multiple_of(j * blk, blk)
        kt = kt_ref[j]
        v = v_ref[0, pl.ds(start, blk), :]
        zero = jnp.zeros_like(kt)
        kt2 = jnp.concatenate([jnp.where(first_rows, kt, zero), jnp.where(first_rows, zero, kt)], axis=1)
        kt2 = jnp.concatenate([kt2, bias_ref[0]], axis=0)
        v2 = jnp.concatenate([jnp.where(first_lanes, v, zero), jnp.where(first_lanes, zero, v)], axis=0)
        z2 = jnp.dot(qa_ref[row0:tq, :], kt2, preferred_element_type=F32) * LOG2E
        sp = _softplus_log2(z2)
        if diagonal:
            query = lax.broadcasted_iota(jnp.int32, (rows, 2 * blk), 0)
            key = lax.broadcasted_iota(jnp.int32, (rows, 2 * blk), 1) & (blk - 1)
            strictly_earlier = key < query
            sp = jnp.where(strictly_earlier, sp, 0.0)
        hi, lo = _split2(sp)
        ra = jnp.dot(jnp.concatenate([hi[:, :blk], lo[:, :blk]], axis=1), tri_ref[...],
                     preferred_element_type=F32)
        rb = jnp.dot(jnp.concatenate([hi[:, blk:], lo[:, blk:]], axis=1), tri_ref[...],
                     preferred_element_type=F32)
        suffix = jnp.concatenate([ra[:, :blk], rb[:, :blk]], axis=1)
        total = jnp.concatenate([ra[:, blk:], rb[:, blk:]], axis=1)
        w = jnp.exp2(z2 - suffix - surv_ref[row0:tq, :])
        if diagonal:
            w = jnp.where(strictly_earlier, w, 0.0)
        acc_ref[row0:tq, :] += jnp.dot(w.astype(BF16), v2, preferred_element_type=F32)
        surv_ref[row0:tq, :] += total

    def earlier_body(jj, carry):
        for u in range(band):
            fold((qi - 1 - jj) * band + band - 1 - u, 0, False)
        return carry

    for b in reversed(range(band)):
        fold(qi * band + b, b * blk, True)
    lax.fori_loop(0, qi, earlier_body, 0)
    o_ref[0] = acc_ref[...].astype(o_ref.dtype)


def _suffix_sum_matrix(blk):
    j = jnp.arange(blk)[:, None]
    s = jnp.arange(blk)[None, :]
    tri = (j >= s).astype(BF16)
    half = jnp.concatenate([tri, jnp.ones((blk, blk), BF16)], axis=1)
    return jnp.concatenate([half, half], axis=0)


def sb_attention_prompt(q, ktb, vb, bias):
    b, s, _ = q.shape
    blk = SB_BLOCK
    tq = SB_QUERY_TILE
    assert blk == PAGE_SIZE and s % tq == 0
    n_pairs = SB_HEADS // 2
    parts = jnp.stack(_exact_bf16_parts(bias.astype(F32)))
    parts = jnp.repeat(parts.reshape(BIAS_PARTS, n_pairs, 2), blk, axis=2).transpose(1, 0, 2)
    bias_rows = jnp.pad(parts, ((0, 0), (0, LANES - BIAS_PARTS), (0, 0))).astype(BF16)
    return pl.pallas_call(
        _sb_prompt_kernel,
        grid=(b, n_pairs, s // tq),
        in_specs=[pl.BlockSpec((1, tq, LANES), lambda bi, hp, qi: (bi, qi, hp)),
                  pl.BlockSpec((s // blk, LANES, blk), lambda bi, hp, qi: (bi, hp, 0)),
                  pl.BlockSpec((1, s, LANES), lambda bi, hp, qi: (bi, 0, hp)),
                  pl.BlockSpec((1, LANES, 2 * blk), lambda bi, hp, qi: (hp, 0, 0)),
                  pl.BlockSpec((2 * blk, 2 * blk), lambda bi, hp, qi: (0, 0))],
        out_specs=pl.BlockSpec((1, tq, LANES), lambda bi, hp, qi: (bi, qi, hp)),
        out_shape=jax.ShapeDtypeStruct((b, s, D_MODEL), BF16),
        scratch_shapes=[pltpu.VMEM((tq, 2 * LANES), BF16), pltpu.VMEM((tq, 2 * blk), F32),
                        pltpu.VMEM((tq, LANES), F32)],
        compiler_params=_params("parallel", "parallel", "arbitrary"),
        name="sb_attention_prompt",
    )(q, ktb, vb, bias_rows, _suffix_sum_matrix(blk))


def _sb_sample_kernel(n_new, pt_ref, qbd_ref, bias_ref, kn_ref, vn_ref, tri_ref, *rest):
    n_pg = PAGES_PER_STEP
    k_refs = rest[:n_pg]
    v_refs = rest[n_pg:2 * n_pg]
    o_ref, surv_ref, acc_ref = rest[2 * n_pg:]
    step = pl.program_id(1)
    qbd = qbd_ref[0]
    bias = bias_ref[...]
    pg = PAGE_SIZE
    rows = qbd.shape[0]

    def scores(kt, visible):
        z2 = jnp.dot(qbd, kt, preferred_element_type=F32) * LOG2E + bias
        sp = _softplus_log2(z2)
        if visible is not None:
            sp = jnp.where(visible, sp, 0.0)
        hi, lo = _split2(sp)
        r = jnp.dot(jnp.concatenate([hi, lo], axis=1), tri_ref[...], preferred_element_type=F32)
        return z2, r[:, :pg], r[:, pg:]

    def weighted_values(z2, suffix, surv, vt, visible):
        w = jnp.exp2(z2 - suffix - surv)
        if visible is not None:
            w = jnp.where(visible, w, 0.0)
        return lax.dot_general(w.astype(BF16), vt, NT_DIMS, preferred_element_type=F32)

    @pl.when(step == 0)
    def _():
        query = lax.broadcasted_iota(jnp.int32, (rows, pg), 0) // SB_HEADS
        key = lax.broadcasted_iota(jnp.int32, (rows, pg), 1)
        visible = (key < query) & (key < n_new)
        z2, suffix, total = scores(kn_ref[0], visible)
        acc_ref[...] = weighted_values(z2, suffix, jnp.zeros_like(total), vn_ref[0], visible)
        surv_ref[...] = total

    parts = [scores(k_refs[i][0].astype(BF16), None) for i in range(n_pg)]
    surv = surv_ref[...]
    acc = acc_ref[...]
    for i, (z2, suffix, total) in enumerate(parts):
        acc = acc + weighted_values(z2, suffix, surv, v_refs[i][0].astype(BF16), None)
        surv = surv + total
    acc_ref[...] = acc
    surv_ref[...] = surv

    @pl.when(step == pl.num_programs(1) - 1)
    def _():
        head_of_row = lax.broadcasted_iota(jnp.int32, (SB_HEADS, D_MODEL), 0)
        head_of_lane = lax.broadcasted_iota(jnp.int32, (SB_HEADS, D_MODEL), 1) // SB_HEAD_DIM
        out = []
        for t in range(o_ref.shape[1]):
            if t < n_new:
                blk = acc_ref[t * SB_HEADS:(t + 1) * SB_HEADS, :]
                out.append(jnp.sum(jnp.where(head_of_row == head_of_lane, blk, 0.0), axis=0, keepdims=True))
            else:
                out.append(jnp.zeros((1, D_MODEL), F32))
        o_ref[0] = jnp.concatenate(out, axis=0).astype(o_ref.dtype)


def sb_attention_sample(q, kt_new, vt_new, cache_kt, cache_vt, page_table, bias):
    b, t, _ = q.shape
    n_pages = page_table.shape[1]
    n_pg = PAGES_PER_STEP
    rows = t * SB_HEADS
    assert n_pages % n_pg == 0 and rows % 16 == 0 and t <= 8
    t_pad = 8
    q4 = q.reshape(b, t, SB_HEADS, SB_HEAD_DIM)
    eye = jnp.eye(SB_HEADS, dtype=BF16)
    qbd = jnp.einsum("bthd,hg->bthgd", q4, eye).reshape(b, rows, D_MODEL)
    bias_rows = jnp.tile(bias.astype(F32) * LOG2E, t).reshape(rows, 1)
    new_pages = lambda a: jnp.pad(a.reshape(D_MODEL, b, t).transpose(1, 0, 2),
                                  ((0, 0), (0, 0), (0, PAGE_SIZE - t))).astype(BF16)

    def page_map(i):
        return lambda bi, si, pt: (pt[bi * n_pages + (n_pages - 1 - (si * n_pg + i))], 0, 0)

    per_b = lambda bi, si, pt: (bi, 0, 0)
    fixed = lambda bi, si, pt: (0, 0)
    page_specs = [pl.BlockSpec((1, D_MODEL, PAGE_SIZE), page_map(i)) for i in range(n_pg)]
    grid_spec = pltpu.PrefetchScalarGridSpec(
        num_scalar_prefetch=1,
        grid=(b, n_pages // n_pg),
        in_specs=[pl.BlockSpec((1, rows, D_MODEL), per_b),
                  pl.BlockSpec((rows, 1), fixed),
                  pl.BlockSpec((1, D_MODEL, PAGE_SIZE), per_b),
                  pl.BlockSpec((1, D_MODEL, PAGE_SIZE), per_b),
                  pl.BlockSpec((2 * PAGE_SIZE, 2 * PAGE_SIZE), fixed)] + page_specs + page_specs,
        out_specs=pl.BlockSpec((1, t_pad, D_MODEL), per_b),
        scratch_shapes=[pltpu.VMEM((rows, PAGE_SIZE), F32), pltpu.VMEM((rows, D_MODEL), F32)],
    )
    o = pl.pallas_call(
        functools.partial(_sb_sample_kernel, t),
        grid_spec=grid_spec,
        out_shape=jax.ShapeDtypeStruct((b, t_pad, D_MODEL), BF16),
        compiler_params=_params("parallel", "arbitrary"),
        name="sb_attention_sample",
    )(page_table.reshape(-1).astype(jnp.int32), qbd, bias_rows, new_pages(kt_new), new_pages(vt_new),
      _suffix_sum_matrix(PAGE_SIZE), *([cache_kt] * n_pg), *([cache_vt] * n_pg))
    return o[:, :t]


def _ssd_kernel(valid_len, has_init,
                x_ref, bm_ref, cm_ref, z_ref, dt_ref,
                cwx_ref, cwb_ref, cwc_ref, cbx_ref, cbb_ref, cbc_ref,
                px_ref, pb_ref, pc_ref, init_ref,
                dtb_ref, a_ref, aexp_ref, dexp_ref, nw_ref, expand_ref,
                y_ref, fin_ref,
                ext_x, ext_b, ext_c, state_ref):
    lc = x_ref.shape[1]
    c = pl.program_id(2)
    tail = 8

    @pl.when(c == 0)
    def _():
        ext_x[0:tail, :] = px_ref[0]
        ext_b[0:tail, :] = pb_ref[0]
        ext_c[0:tail, :] = pc_ref[0]
        if has_init:
            n = SSM_HEADS_PER_GROUP * SSM_HEAD_DIM
            state_ref[...] = init_ref[0].reshape(n, SSM_D_STATE).T
        else:
            state_ref[...] = jnp.zeros_like(state_ref)

    def conv_silu(cur_ref, ext, w_ref, b_ref):
        ext[tail:tail + lc, :] = cur_ref[0]
        acc = b_ref[...] + w_ref[SSM_CONV_W - 1:SSM_CONV_W, :] * ext[tail:tail + lc, :]
        for back in range(1, SSM_CONV_W):
            tap = SSM_CONV_W - 1 - back
            acc = acc + w_ref[tap:tap + 1, :] * ext[tail - back:tail - back + lc, :]
        ext[0:tail, :] = ext[lc:lc + tail, :]
        return acc * _sigmoid(acc)

    xs = conv_silu(x_ref, ext_x, cwx_ref, cbx_ref)
    bm = conv_silu(bm_ref, ext_b, cwb_ref, cbb_ref)
    cm = conv_silu(cm_ref, ext_c, cwc_ref, cbc_ref)

    dt = _softplus(dt_ref[0] + dtb_ref[0])
    if valid_len < lc:
        dt = jnp.where(lax.broadcasted_iota(jnp.int32, dt.shape, 0) < valid_len, dt, 0.0)
    t_idx = lax.broadcasted_iota(jnp.int32, (lc, lc), 0)
    s_idx = lax.broadcasted_iota(jnp.int32, (lc, lc), 1)
    causal = s_idx <= t_idx
    lower = causal.astype(BF16)

    def cumsum_rows(v):
        parts = _split3(v)
        return sum(jnp.dot(lower, p, preferred_element_type=F32) for p in parts)

    a_cum = cumsum_rows(dt * a_ref[0])
    a_cum_t = a_cum.T
    expand = expand_ref[...]
    dt_wide = sum(jnp.dot(p, expand, preferred_element_type=F32) for p in _split3(dt))
    a_cum_wide = cumsum_rows(dt_wide * aexp_ref[...])
    a_last_wide = a_cum_wide[lc - 1:lc, :]

    bm_t = bm.T.astype(BF16)
    cm_b = cm.astype(BF16)
    cb = jnp.dot(cm_b, bm_t, preferred_element_type=F32)
    xdt = xs * dt_wide
    xdt_b = xdt.astype(BF16)
    first_half = lax.broadcasted_iota(jnp.int32, (lc, LANES), 1) < SSM_HEAD_DIM
    y_parts = []
    for pair in range(SSM_HEADS_PER_GROUP // 2):
        halves = []
        for k in (2 * pair, 2 * pair + 1):
            seg = a_cum[:, k:k + 1] - a_cum_t[k:k + 1, :]
            m = (cb * jnp.exp(jnp.where(causal, seg, -jnp.inf))).astype(BF16)
            halves.append(jnp.dot(m, xdt_b[:, pair * LANES:(pair + 1) * LANES], preferred_element_type=F32))
        y_parts.append(jnp.where(first_half, halves[0], halves[1]))
    y = jnp.concatenate(y_parts, axis=1)

    state = state_ref[...]
    y = y + jnp.dot(cm_b, state.astype(BF16), preferred_element_type=F32) * jnp.exp(a_cum_wide)
    y = y + dexp_ref[...] * xs
    y = y * (z_ref[0] * _sigmoid(z_ref[0]))
    y = y * lax.rsqrt(jnp.mean(y * y, axis=-1, keepdims=True) + RMS_EPS) * nw_ref[...]
    y_ref[0] = y.astype(y_ref.dtype)

    x_to_end = (xdt * jnp.exp(a_last_wide - a_cum_wide)).astype(BF16)
    new_state = state * jnp.exp(a_last_wide) + jnp.dot(bm_t, x_to_end, preferred_element_type=F32)
    state_ref[...] = new_state

    @pl.when(c == pl.num_programs(2) - 1)
    def _():
        fin_ref[0] = new_state.T.reshape(SSM_HEADS_PER_GROUP, SSM_HEAD_DIM, SSM_D_STATE)


def ssd_mixer(z, xbc, dt, conv_prev, init_state, valid_len, p):
    b, l, _ = z.shape
    lc = SSM_CHUNK
    assert l % lc == 0
    n_x = SSM_D_INNER // SSM_GROUP_DIM
    gd, ns = SSM_GROUP_DIM, SSM_D_STATE
    b_off = SSM_D_INNER // ns
    c_off = b_off + SSM_GROUPS
    has_init = init_state is not None
    if not has_init:
        init_state = jnp.zeros((1, SSM_HEADS, SSM_HEAD_DIM, ns), F32)

    seq = lambda off: (lambda bi, g, c: (bi, c, off + g))
    par = lambda off: (lambda bi, g, c: (0, off + g))
    prev = lambda off: (lambda bi, g, c: (bi, 0, off + g))
    init_map = (lambda bi, g, c: (bi, g, 0, 0)) if has_init else (lambda bi, g, c: (0, g, 0, 0))
    hg = SSM_HEADS_PER_GROUP
    in_specs = [
        pl.BlockSpec((1, lc, gd), seq(0)), pl.BlockSpec((1, lc, ns), seq(b_off)), pl.BlockSpec((1, lc, ns), seq(c_off)),
        pl.BlockSpec((1, lc, gd), seq(0)), pl.BlockSpec((1, lc, LANES), seq(0)),
        pl.BlockSpec((SSM_CONV_W, gd), par(0)), pl.BlockSpec((SSM_CONV_W, ns), par(b_off)),
        pl.BlockSpec((SSM_CONV_W, ns), par(c_off)),
        pl.BlockSpec((1, gd), par(0)), pl.BlockSpec((1, ns), par(b_off)), pl.BlockSpec((1, ns), par(c_off)),
        pl.BlockSpec((1, 8, gd), prev(0)), pl.BlockSpec((1, 8, ns), prev(b_off)), pl.BlockSpec((1, 8, ns), prev(c_off)),
        pl.BlockSpec((1, hg, SSM_HEAD_DIM, ns), init_map),
        pl.BlockSpec((1, 1, LANES), lambda bi, g, c: (g, 0, 0)),
        pl.BlockSpec((1, 1, LANES), lambda bi, g, c: (g, 0, 0)),
        pl.BlockSpec((1, gd), par(0)), pl.BlockSpec((1, gd), par(0)), pl.BlockSpec((1, gd), par(0)),
        pl.BlockSpec((LANES, gd), lambda bi, g, c: (0, 0)),
    ]
    out_specs = [pl.BlockSpec((1, lc, gd), seq(0)),
                 pl.BlockSpec((1, hg, SSM_HEAD_DIM, ns), lambda bi, g, c: (bi, g, 0, 0))]
    return pl.pallas_call(
        functools.partial(_ssd_kernel, valid_len, has_init),
        grid=(b, SSM_GROUPS, l // lc),
        in_specs=in_specs,
        out_specs=out_specs,
        out_shape=[jax.ShapeDtypeStruct((b, l, SSM_D_INNER), BF16),
                   jax.ShapeDtypeStruct((b, SSM_HEADS, SSM_HEAD_DIM, ns), F32)],
        scratch_shapes=[pltpu.VMEM((lc + 8, gd), F32), pltpu.VMEM((lc + 8, ns), F32),
                        pltpu.VMEM((lc + 8, ns), F32), pltpu.VMEM((ns, gd), F32)],
        compiler_params=_params("parallel", "parallel", "arbitrary"),
        name="ssd_mixer",
    )(xbc, xbc, xbc, z, dt,
      p["conv_w"], p["conv_w"], p["conv_w"], p["conv_b"], p["conv_b"], p["conv_b"],
      conv_prev, conv_prev, conv_prev, init_state,
      p["dt_bias"], p["a_neg"], p["a_wide"], p["d_wide"], p["norm_w"], p["expand"])


def ssm_params(w_in, conv_w, conv_b, dt_bias, a_log, d_skip, norm_w, w_out):
    hg = SSM_HEADS_PER_GROUP
    w_dt = w_in[:, SSM_D_INNER + SSM_CONV_DIM:].reshape(D_MODEL, SSM_GROUPS, hg)
    w_dt = jnp.pad(w_dt, ((0, 0), (0, 0), (0, LANES - hg))).reshape(D_MODEL, SSM_DT_COLS)
    lane_pad = lambda v: jnp.pad(v.astype(F32).reshape(SSM_GROUPS, 1, hg), ((0, 0), (0, 0), (0, LANES - hg)))
    a_neg = -jnp.exp(a_log.astype(F32))
    head_of_chan = jnp.arange(SSM_GROUP_DIM) // SSM_HEAD_DIM
    expand = (jnp.arange(LANES)[:, None] == head_of_chan[None, :]).astype(BF16)
    return {
        "w_z": w_in[:, :SSM_D_INNER].astype(BF16),
        "w_xbc": w_in[:, SSM_D_INNER:SSM_D_INNER + SSM_CONV_DIM].astype(BF16),
        "w_dt": w_dt.astype(BF16),
        "conv_w": conv_w.astype(F32), "conv_b": conv_b.astype(F32).reshape(1, SSM_CONV_DIM),
        "dt_bias": lane_pad(dt_bias), "a_neg": lane_pad(a_neg),
        "a_wide": jnp.repeat(a_neg, SSM_HEAD_DIM).reshape(1, SSM_D_INNER),
        "d_wide": jnp.repeat(d_skip.astype(F32), SSM_HEAD_DIM).reshape(1, SSM_D_INNER),
        "norm_w": norm_w.astype(F32).reshape(1, SSM_D_INNER),
        "expand": expand,
        "w_out": w_out.astype(BF16),
    }


def _router_kernel(h_ref, g_ref, w_ref, b_ref, before_ref, xn_ref, meta_ref, cnt_ref, carry_ref):
    step = pl.program_id(0)
    tm = h_ref.shape[0]

    @pl.when(step == 0)
    def _():
        carry_ref[...] = jnp.zeros_like(carry_ref)

    xn = _rms_norm(h_ref[...], g_ref[...])
    xn_ref[...] = xn
    logits = lax.dot_general(w_ref[...], xn.astype(BF16), NT_DIMS, preferred_element_type=F32) + b_ref[...]

    sub8 = lax.broadcasted_iota(jnp.int32, (8, tm), 0)
    neg_inf = jnp.float32(-jnp.inf)

    def first_argmax(v, vmax):
        return jnp.min(jnp.where(v == vmax, sub8, 8), axis=0, keepdims=True)

    grp = jnp.where(sub8 < MOE_GROUPS, logits[0:8], neg_inf)
    g_max = jnp.max(grp, axis=0, keepdims=True)
    g_idx = first_argmax(grp, g_max)
    g_prob = 1.0 / jnp.sum(jnp.exp(grp - g_max), axis=0, keepdims=True)

    in_group = logits[8:16]
    for g in range(1, MOE_GROUPS):
        in_group = jnp.where(g_idx == g, logits[8 + 8 * g:16 + 8 * g], in_group)
    m1 = jnp.max(in_group, axis=0, keepdims=True)
    i1 = first_argmax(in_group, m1)
    rest = jnp.where(sub8 == i1, neg_inf, in_group)
    m2 = jnp.max(rest, axis=0, keepdims=True)
    i2 = first_argmax(rest, m2)
    p2 = jnp.exp(m2 - m1)
    w1 = g_prob / (1.0 + p2)
    w2 = g_prob * p2 / (1.0 + p2)
    e1 = g_idx * MOE_EXPERTS_PER_GROUP + i1
    e2 = g_idx * MOE_EXPERTS_PER_GROUP + i2

    sub_e = lax.broadcasted_iota(jnp.int32, (MOE_EXPERTS, tm), 0)
    hot1 = sub_e == e1
    hot2 = sub_e == e2
    hot = jnp.concatenate([hot1, hot2], axis=0).astype(BF16)
    earlier = jnp.dot(hot, before_ref[...], preferred_element_type=F32)
    hot1f = hot1.astype(F32)
    hot2f = hot2.astype(F32)
    cnt1 = jnp.sum(hot1f, axis=1, keepdims=True)
    cnt2 = jnp.sum(hot2f, axis=1, keepdims=True)
    carry = carry_ref[...]
    r1 = jnp.sum(hot1f * (earlier[:MOE_EXPERTS] + carry), axis=0, keepdims=True)
    r2 = jnp.sum(hot2f * (earlier[MOE_EXPERTS:] + carry + cnt1), axis=0, keepdims=True)
    carry = carry + cnt1 + cnt2
    carry_ref[...] = carry
    cnt_ref[...] = carry
    zero = jnp.zeros((1, tm), F32)
    meta_ref[...] = jnp.concatenate(
        [e1.astype(F32), e2.astype(F32), w1, w2, r1, r2, zero, zero], axis=0)


def moe_route(h, gain, r):
    t = h.shape[0]
    tm = _row_tile(t, 512)
    i = jnp.arange(tm)
    before = (i[:, None] < i[None, :]).astype(BF16)
    fixed = lambda s: (0, 0)
    return pl.pallas_call(
        _router_kernel,
        grid=(t // tm,),
        in_specs=[pl.BlockSpec((tm, D_MODEL), lambda s: (s, 0)),
                  pl.BlockSpec((1, D_MODEL), fixed),
                  pl.BlockSpec((ROUTE_ROWS, D_MODEL), fixed),
                  pl.BlockSpec((ROUTE_ROWS, 1), fixed),
                  pl.BlockSpec((tm, tm), fixed)],
        out_specs=[pl.BlockSpec((tm, D_MODEL), lambda s: (s, 0)),
                   pl.BlockSpec((8, tm), lambda s: (0, s)),
                   pl.BlockSpec((MOE_EXPERTS, 1), fixed)],
        out_shape=[jax.ShapeDtypeStruct((t, D_MODEL), F32),
                   jax.ShapeDtypeStruct((8, t), F32),
                   jax.ShapeDtypeStruct((MOE_EXPERTS, 1), F32)],
        scratch_shapes=[pltpu.VMEM((MOE_EXPERTS, 1), F32)],
        compiler_params=_params("arbitrary"),
        name="moe_route",
    )(h, gain, r["w_route"], r["bias"], before)


def _scatter_rows_kernel(pend_ref, pos_ref, x_ref, rows_ref, zeros, sem, zero_sem):
    tm = x_ref.shape[0]
    br = zeros.shape[0]

    @pl.when(pl.program_id(0) == 0)
    def _():
        zeros[...] = jnp.zeros_like(zeros)

        def last_block(e):
            start = pl.multiple_of(pend_ref[e] - br, br)
            return pltpu.make_async_copy(zeros, rows_ref.at[pl.ds(start, br), :], zero_sem)

        def has_rows(e):
            return pend_ref[e] > (pend_ref[e - 1] if e > 0 else 0)

        n_blocks = rows_ref.shape[0] // br
        n_used = pend_ref[MOE_EXPERTS - 1] // br

        def spare_block(k):
            return pltpu.make_async_copy(zeros, rows_ref.at[pl.ds((n_blocks - 1 - k) * br, br), :], zero_sem)

        for e in range(MOE_EXPERTS):
            @pl.when(has_rows(e))
            def _():
                last_block(e).start()

            @pl.when(n_blocks - 1 - e >= n_used)
            def _():
                spare_block(e).start()

        for e in range(MOE_EXPERTS):
            @pl.when(has_rows(e))
            def _():
                last_block(e).wait()

            @pl.when(n_blocks - 1 - e >= n_used)
            def _():
                spare_block(e).wait()

    def copy(k, t):
        return pltpu.make_async_copy(x_ref.at[pl.ds(t, 1), :], rows_ref.at[pl.ds(pos_ref[k, t], 1), :], sem)

    def start(t, carry):
        copy(0, t).start()
        copy(1, t).start()
        return carry

    def wait(t, carry):
        copy(0, t).wait()
        copy(1, t).wait()
        return carry

    lax.fori_loop(0, tm, start, 0, unroll=ROW_DMA_UNROLL)
    lax.fori_loop(0, tm, wait, 0, unroll=ROW_DMA_UNROLL)


def moe_scatter_rows(xn, pos, padded_end, n_rows):
    t = xn.shape[0]
    tm = _row_tile(t, 512)
    grid_spec = pltpu.PrefetchScalarGridSpec(
        num_scalar_prefetch=1,
        grid=(t // tm,),
        in_specs=[pl.BlockSpec((2, tm), lambda s, pe: (0, s), memory_space=pltpu.SMEM),
                  pl.BlockSpec((tm, D_MODEL), lambda s, pe: (s, 0))],
        out_specs=pl.BlockSpec(memory_space=pl.ANY),
        scratch_shapes=[pltpu.VMEM((MOE_BLOCK_ROWS, D_MODEL), F32), pltpu.SemaphoreType.DMA(()),
                        pltpu.SemaphoreType.DMA(())],
    )
    return pl.pallas_call(
        _scatter_rows_kernel,
        grid_spec=grid_spec,
        out_shape=jax.ShapeDtypeStruct((n_rows, D_MODEL), F32),
        compiler_params=_params("arbitrary"),
        name="moe_scatter_rows",
    )(padded_end, pos, xn)


def _expert_mlp_kernel(be_ref, nb_ref, x_ref, wgu_ref, wd_ref, o_ref, wgu_bf, wd_bf):
    i = pl.program_id(0)

    @pl.when(i < nb_ref[0])
    def _():
        @pl.when((i == 0) | (be_ref[i] != be_ref[jnp.maximum(i - 1, 0)]))
        def _():
            wgu_bf[...] = wgu_ref[0, 0].astype(BF16)
            wd_bf[...] = wd_ref[0, 0].astype(BF16)

        x = x_ref[...].astype(BF16)
        gate = jnp.dot(x, wgu_bf[:, :MOE_D_FF], preferred_element_type=F32)
        up = jnp.dot(x, wgu_bf[:, MOE_D_FF:], preferred_element_type=F32)
        act = (gate * _sigmoid(gate) * up).astype(BF16)
        o_ref[...] = jnp.dot(act, wd_bf[...], preferred_element_type=F32)

    @pl.when(i >= nb_ref[0])
    def _():
        o_ref[...] = jnp.zeros_like(o_ref)


def moe_expert_mlp(rows, block_expert, n_used, w_gu, w_down, layer):
    n_rows = rows.shape[0]
    br = MOE_BLOCK_ROWS
    grid_spec = pltpu.PrefetchScalarGridSpec(
        num_scalar_prefetch=2,
        grid=(n_rows // br,),
        in_specs=[pl.BlockSpec((br, D_MODEL), lambda i, be, nb: (i, 0)),
                  pl.BlockSpec((1, 1, D_MODEL, 2 * MOE_D_FF), lambda i, be, nb: (layer, be[i], 0, 0)),
                  pl.BlockSpec((1, 1, MOE_D_FF, D_MODEL), lambda i, be, nb: (layer, be[i], 0, 0))],
        out_specs=pl.BlockSpec((br, D_MODEL), lambda i, be, nb: (i, 0)),
        scratch_shapes=[pltpu.VMEM((D_MODEL, 2 * MOE_D_FF), BF16), pltpu.VMEM((MOE_D_FF, D_MODEL), BF16)],
    )
    return pl.pallas_call(
        _expert_mlp_kernel,
        grid_spec=grid_spec,
        out_shape=jax.ShapeDtypeStruct((n_rows, D_MODEL), F32),
        compiler_params=_params("arbitrary"),
        name="moe_expert_mlp",
    )(block_expert, n_used, rows, w_gu, w_down)


def _combine_kernel(final_norm, pos_ref, h_ref, w_ref, rows_ref, g_ref, o_ref, y_ref, buf1, buf2, sem):
    tm = h_ref.shape[0]

    def copies(t):
        return (pltpu.make_async_copy(rows_ref.at[pl.ds(pos_ref[0, t], 1), :], buf1.at[pl.ds(t, 1), :], sem),
                pltpu.make_async_copy(rows_ref.at[pl.ds(pos_ref[1, t], 1), :], buf2.at[pl.ds(t, 1), :], sem))

    def start(t, carry):
        c1, c2 = copies(t)
        c1.start()
        c2.start()
        return carry

    def wait(t, carry):
        c1, c2 = copies(t)
        c1.wait()
        c2.wait()
        return carry

    lax.fori_loop(0, tm, start, 0, unroll=ROW_DMA_UNROLL)
    lax.fori_loop(0, tm, wait, 0, unroll=ROW_DMA_UNROLL)
    w = w_ref[...]
    out = h_ref[...] + (w[:, 0:1] * buf1[...] + w[:, 1:2] * buf2[...])
    o_ref[...] = out
    if final_norm:
        y_ref[...] = _rms_norm(out, g_ref[...])
    else:
        y_ref[...] = jnp.zeros_like(y_ref)


def moe_combine(h, pos, w_cols, out_rows, final_gain):
    t = h.shape[0]
    tm = _row_tile(t, 512)
    final_norm = final_gain is not None
    gain = final_gain if final_norm else jnp.ones((1, D_MODEL), F32)
    y_rows = tm if final_norm else 8
    y_shape = (t, D_MODEL) if final_norm else (8 * (t // tm), D_MODEL)
    out, y = pl.pallas_call(
        functools.partial(_combine_kernel, final_norm),
        grid=(t // tm,),
        in_specs=[pl.BlockSpec((2, tm), lambda s: (0, s), memory_space=pltpu.SMEM),
                  pl.BlockSpec((tm, D_MODEL), lambda s: (s, 0)),
                  pl.BlockSpec((tm, 2), lambda s: (s, 0)),
                  pl.BlockSpec(memory_space=pl.ANY),
                  pl.BlockSpec((1, D_MODEL), lambda s: (0, 0))],
        out_specs=[pl.BlockSpec((tm, D_MODEL), lambda s: (s, 0)),
                   pl.BlockSpec((y_rows, D_MODEL), lambda s: (s, 0))],
        out_shape=[jax.ShapeDtypeStruct((t, D_MODEL), F32), jax.ShapeDtypeStruct(y_shape, F32)],
        scratch_shapes=[pltpu.VMEM((tm, D_MODEL), F32), pltpu.VMEM((tm, D_MODEL), F32),
                        pltpu.SemaphoreType.DMA(())],
        compiler_params=_params("arbitrary"),
        name="moe_combine",
    )(pos, h, w_cols, out_rows, gain)
    return out, (y if final_norm else None)


def moe_params(w_group, b_group, w_route, b_route):
    w_rt = jnp.zeros((ROUTE_ROWS, D_MODEL), F32)
    w_rt = w_rt.at[0:MOE_GROUPS].set(w_group.astype(F32).T)
    w_rt = w_rt.at[8:8 + MOE_EXPERTS].set(w_route.astype(F32).T)
    bias = jnp.zeros((ROUTE_ROWS,), F32)
    bias = bias.at[0:MOE_GROUPS].set(b_group.astype(F32)).at[8:8 + MOE_EXPERTS].set(b_route.astype(F32))
    return {"w_route": w_rt.astype(BF16), "bias": bias.reshape(ROUTE_ROWS, 1)}


def hier_moe_residual(h, gain, r, w_gu, w_down, layer, final_gain=None):
    t = h.shape[0]
    br = MOE_BLOCK_ROWS
    xn, meta, counts = moe_route(h, gain, r)
    counts = counts[:, 0].astype(jnp.int32)
    padded = (counts + br - 1) // br * br
    padded_end = jnp.cumsum(padded)
    padded_start = padded_end - padded
    expert = meta[0:2].astype(jnp.int32)
    hot = expert[..., None] == jnp.arange(MOE_EXPERTS, dtype=jnp.int32)
    pos = jnp.sum(jnp.where(hot, padded_start, 0), axis=-1) + meta[4:6].astype(jnp.int32)
    n_blocks = -(-(t * MOE_TOP_K) // br) + MOE_EXPERTS
    block_start = jnp.arange(n_blocks, dtype=jnp.int32) * br
    block_expert = jnp.minimum(jnp.sum(padded_end[None, :] <= block_start[:, None], axis=1), MOE_EXPERTS - 1)
    n_used = (padded_end[-1] // br).reshape(1)
    rows = moe_scatter_rows(xn, pos, padded_end.astype(jnp.int32), n_blocks * br)
    out_rows = moe_expert_mlp(rows, block_expert.astype(jnp.int32), n_used.astype(jnp.int32), w_gu, w_down, layer)
    return moe_combine(h, pos, meta[2:4].T, out_rows, final_gain)


def kernel(x_prompt, x_sample, cache_k, cache_v, page_table, state_ssm, state_conv, norm_mix, norm_ffn, norm_final,
           sb_w_qkv, sb_bias, sb_w_o, ssm_w_in, ssm_conv_w, ssm_conv_b, ssm_dt_bias, ssm_a_log, ssm_d, ssm_norm_w,
           ssm_w_out, moe_w_group, moe_b_group, moe_w_route, moe_b_route, moe_w_gu, moe_w_down):
    bp, sp, _ = x_prompt.shape
    bs, ts, _ = x_sample.shape
    depth = norm_mix.shape[0]
    hp = x_prompt.reshape(bp * sp, D_MODEL)
    hs = x_sample.reshape(bs * ts, D_MODEL)
    n_phys = cache_k.shape[1]
    gain = lambda g: g.astype(F32).reshape(1, D_MODEL)
    outs = {name: [] for name in ("k_p", "v_p", "k_s", "v_s", "ssm_p", "conv_p", "ssm_s", "conv_s")}
    yp = ys = None
    for layer in range(depth):
        slot = layer // N_MIXERS
        g_mix = gain(norm_mix[layer])
        if layer % N_MIXERS == 0:
            w = sb_w_qkv[slot]
            w_q = w[:, :D_MODEL].astype(BF16)
            w_kv_t = w[:, D_MODEL:].T.astype(BF16)
            w_o = sb_w_o[slot].astype(BF16)
            q, kt, vt, ktb, vb = qkv_project(hp, g_mix, w_q, w_kv_t)
            shape3 = (bp, sp, D_MODEL)
            o = sb_attention_prompt(q.reshape(shape3), ktb, vb.reshape(shape3), sb_bias[slot])
            hp = out_project_residual(hp, o.reshape(bp * sp, D_MODEL), w_o)
            pages = lambda a: a.reshape(bp, sp // PAGE_SIZE, SB_HEADS, SB_HEAD_DIM, PAGE_SIZE).transpose(0, 1, 4, 2, 3)
            outs["k_p"].append(pages(kt))
            outs["v_p"].append(pages(vt))

            q, kt, vt, _, _ = qkv_project(hs, g_mix, w_q, w_kv_t)
            kt, vt = kt[0], vt[0]
            cache_t = lambda c: c[slot].transpose(0, 2, 3, 1).reshape(n_phys, D_MODEL, PAGE_SIZE)
            o = sb_attention_sample(q.reshape(bs, ts, D_MODEL), kt, vt, cache_t(cache_k), cache_t(cache_v),
                                    page_table, sb_bias[slot])
            hs = out_project_residual(hs, o.reshape(bs * ts, D_MODEL), w_o)
            rows = lambda a: a.reshape(SB_HEADS, SB_HEAD_DIM, bs, ts).transpose(2, 3, 0, 1)
            outs["k_s"].append(rows(kt))
            outs["v_s"].append(rows(vt))
        else:
            p = ssm_params(ssm_w_in[slot], ssm_conv_w[slot], ssm_conv_b[slot], ssm_dt_bias[slot], ssm_a_log[slot],
                           ssm_d[slot], ssm_norm_w[slot], ssm_w_out[slot])
            z, xbc, dt = ssm_in_project(hp, g_mix, p["w_z"], p["w_xbc"], p["w_dt"])
            xbc3 = xbc.reshape(bp, sp, SSM_CONV_DIM)
            y, fin = ssd_mixer(z.reshape(bp, sp, SSM_D_INNER), xbc3, dt.reshape(bp, sp, SSM_DT_COLS),
                               jnp.zeros((bp, 8, SSM_CONV_DIM), F32), None, SSM_CHUNK, p)
            hp = out_project_residual(hp, y.reshape(bp * sp, SSM_D_INNER), p["w_out"])
            outs["ssm_p"].append(fin)
            outs["conv_p"].append(xbc3[:, sp - (SSM_CONV_W - 1):])

            z, xbc, dt = ssm_in_project(hs, g_mix, p["w_z"], p["w_xbc"], p["w_dt"])
            pad_seq = lambda a: jnp.pad(a.reshape(bs, ts, -1), ((0, 0), (0, SSM_CHUNK - ts), (0, 0)))
            conv_prev = jnp.pad(state_conv[slot].astype(F32), ((0, 0), (8 - (SSM_CONV_W - 1), 0), (0, 0)))
            y, fin = ssd_mixer(pad_seq(z), pad_seq(xbc), pad_seq(dt), conv_prev, state_ssm[slot].astype(F32), ts, p)
            hs = out_project_residual(hs, y[:, :ts].reshape(bs * ts, SSM_D_INNER), p["w_out"])
            xbc_ext = jnp.concatenate([state_conv[slot].astype(F32), xbc.reshape(bs, ts, SSM_CONV_DIM)], axis=1)
            outs["ssm_s"].append(fin)
            outs["conv_s"].append(xbc_ext[:, ts:])
        r = moe_params(moe_w_group[layer], moe_b_group[layer], moe_w_route[layer], moe_b_route[layer])
        final_gain = gain(norm_final) if layer == depth - 1 else None
        hp, yp = hier_moe_residual(hp, gain(norm_ffn[layer]), r, moe_w_gu, moe_w_down, layer, final_gain)
        hs, ys = hier_moe_residual(hs, gain(norm_ffn[layer]), r, moe_w_gu, moe_w_down, layer, final_gain)
    return (yp.reshape(bp, sp, D_MODEL), ys.reshape(bs, ts, D_MODEL),
            jnp.stack(outs["k_p"]), jnp.stack(outs["v_p"]), jnp.stack(outs["k_s"]), jnp.stack(outs["v_s"]),
            jnp.stack(outs["ssm_p"]), jnp.stack(outs["conv_p"]), jnp.stack(outs["ssm_s"]), jnp.stack(outs["conv_s"]))
```
